```python
import math
import jax, jax.numpy as jnp
from jax import lax
import numpy as np


D_MODEL = 1024
BATCH = 16
SEQ = 4096
DEPTH = 2
DEC_BATCH = 8
DEC_SEQ = 4096
PAST_LEN = 128

N_MIXERS = 2
N_A_LAYERS = (DEPTH + 1) // 2
N_B_LAYERS = DEPTH // 2
HEAD_DIM = 64
DILATION_PAIRS = ((128, 1), (512, 4), (2048, 16))
N_GROUPS_A = len(DILATION_PAIRS)
HEADS_PER_GROUP = 4
N_HEADS_A = N_GROUPS_A * HEADS_PER_GROUP
ATTN_WIDTH = N_HEADS_A * HEAD_DIM
CHUNK = 128
GATE_WIDTH = D_MODEL
N_SPATIAL_GROUPS = 8
SPATIAL_GROUP_WIDTH = GATE_WIDTH // N_SPATIAL_GROUPS
D_FF = 2816
CONV_WIDTH = 3
EPS = 1e-6
NEG_INF = -1e30

kernel_name = 'hybrid_dilated_attn_chunk_gmlp_encoder'


def rmsnorm(x, g):
    xf = x.astype(jnp.float32)
    y = xf * lax.rsqrt(jnp.mean(xf * xf, axis=-1, keepdims=True) + EPS)
    return (y * g.astype(jnp.float32)).astype(x.dtype)


def alibi_slopes(n):
    return jnp.exp2(-8.0 * jnp.arange(1, n + 1, dtype=jnp.float32) / n)


def _dilated_group(q, k, v, r, half, slopes):
    B, S, H, Dh = q.shape
    L = S // r
    blk = half
    nblk = -(-L // blk)
    Lp = nblk * blk
    pad = Lp - L

    def by_residue(t):
        return t.reshape(B, L, r, H, Dh).transpose(0, 2, 1, 3, 4)

    qr, kr, vr = by_residue(q), by_residue(k), by_residue(v)
    qb = jnp.pad(qr, ((0, 0), (0, 0), (0, pad), (0, 0), (0, 0))).reshape(B, r, nblk, blk, H, Dh)

    def windows(t):
        tp = jnp.pad(t, ((0, 0), (0, 0), (blk, pad + blk), (0, 0), (0, 0)))
        parts = [tp[:, :, o * blk:o * blk + Lp].reshape(B, r, nblk, blk, H, Dh) for o in range(3)]
        return jnp.concatenate(parts, axis=3)

    kw, vw = windows(kr), windows(vr)
    s = jnp.einsum('brnihd,brnmhd->brnhim', qb, kw, preferred_element_type=jnp.float32) * (Dh ** -0.5)

    qi = jnp.arange(blk)[:, None]
    km = jnp.arange(3 * blk)[None, :]
    rel = km - blk - qi
    key_pos = jnp.arange(nblk)[:, None] * blk - blk + jnp.arange(3 * blk)[None, :]
    valid = (jnp.abs(rel) <= half)[None] & ((key_pos >= 0) & (key_pos < L))[:, None, :]
    dist = (r * jnp.abs(rel)).astype(jnp.float32)
    s = s - slopes.astype(jnp.float32)[:, None, None] * dist
    s = jnp.where(valid[:, None], s, NEG_INF)
    lse = jax.nn.logsumexp(s, axis=-1)
    p = jnp.exp(s - lse[..., None]).astype(v.dtype)
    o = jnp.einsum('brnhim,brnmhd->brnihd', p, vw)
    o = o.reshape(B, r, Lp, H, Dh)[:, :, :L].transpose(0, 2, 1, 3, 4).reshape(B, S, H, Dh)
    lse = lse.transpose(0, 1, 2, 4, 3).reshape(B, r, Lp, H)[:, :, :L].transpose(0, 2, 1, 3).reshape(B, S, H)
    return o, lse


def dilated_attention_mixer(h, w_qkv, g_q, g_k, w_o):
    B, S, _ = h.shape
    qkv = h @ w_qkv
    q, k, v = jnp.split(qkv, 3, axis=-1)
    q = rmsnorm(q.reshape(B, S, N_HEADS_A, HEAD_DIM), g_q)
    k = rmsnorm(k.reshape(B, S, N_HEADS_A, HEAD_DIM), g_k)
    v = v.reshape(B, S, N_HEADS_A, HEAD_DIM)
    slopes = alibi_slopes(N_HEADS_A)
    outs, lses = [], []
    for g, (w, r) in enumerate(DILATION_PAIRS):
        sl = slice(g * HEADS_PER_GROUP, (g + 1) * HEADS_PER_GROUP)
        o, l = _dilated_group(q[:, :, sl], k[:, :, sl], v[:, :, sl], r, w // (2 * r), slopes[sl])
        outs.append(o)
        lses.append(l)
    alpha = jax.nn.softmax(jnp.stack(lses, axis=0), axis=0)
    o = jnp.concatenate([outs[g] * alpha[g][..., None].astype(outs[g].dtype) for g in range(N_GROUPS_A)], axis=2)
    return o.reshape(B, S, ATTN_WIDTH) @ w_o


def chunk_gmlp_mixer(h, w_in, g_v, w_s, b_s, w_out):
    B, S, _ = h.shape
    z = jax.nn.gelu(h @ w_in)
    u, v = jnp.split(z, 2, axis=-1)
    v = rmsnorm(v, g_v)
    vc = v.reshape(B, S // CHUNK, CHUNK, N_SPATIAL_GROUPS, SPATIAL_GROUP_WIDTH)
    s = jnp.einsum('gpq,bnqgc->bnpgc', w_s, vc) + b_s.T[:, :, None]
    return (u * s.reshape(B, S, GATE_WIDTH)) @ w_out


def conv_gated_ffn(h, w_up, conv_w, conv_b, w_down):
    a = h @ w_up
    C = a.shape[-1]
    a = lax.conv_general_dilated(a, conv_w[:, None, :].astype(a.dtype), window_strides=(1,),
                                 padding=((CONV_WIDTH // 2, CONV_WIDTH // 2),),
                                 dimension_numbers=('NWC', 'WIO', 'NWC'),
                                 feature_group_count=C) + conv_b
    gate, val = jnp.split(a, 2, axis=-1)
    return (jax.nn.silu(gate) * val) @ w_down


def trunk(x, attn_norm_g, attn_w_qkv, attn_q_norm_g, attn_k_norm_g, attn_w_o,
          gmlp_norm_g, gmlp_w_in, gmlp_v_norm_g, gmlp_w_spatial, gmlp_b_spatial, gmlp_w_out,
          ffn_norm_g, ffn_w_up, ffn_conv_w, ffn_conv_b, ffn_w_down):
    for i in range(DEPTH):
        j = i // N_MIXERS
        if i % N_MIXERS == 0:
            x = x + dilated_attention_mixer(rmsnorm(x, attn_norm_g[j]), attn_w_qkv[j],
                                            attn_q_norm_g[j], attn_k_norm_g[j], attn_w_o[j])
        else:
            x = x + chunk_gmlp_mixer(rmsnorm(x, gmlp_norm_g[j]), gmlp_w_in[j], gmlp_v_norm_g[j],
                                     gmlp_w_spatial[j], gmlp_b_spatial[j], gmlp_w_out[j])
        x = x + conv_gated_ffn(rmsnorm(x, ffn_norm_g[i]), ffn_w_up[i], ffn_conv_w[i], ffn_conv_b[i], ffn_w_down[i])
    return x


def setup_inputs(seed: int = 0) -> dict:
    key = jax.random.key(seed)
    ks = jax.random.split(key, 18)
    f32 = jnp.float32

    def nrm(k, shape, scale):
        return jax.random.normal(k, shape, f32) * scale

    def gain(k, shape):
        return 1.0 + 0.02 * jax.random.normal(k, shape, f32)

    return {
        'x_prompt': nrm(ks[0], (BATCH, SEQ, D_MODEL), 1.0),
        'x_sample': nrm(ks[1], (DEC_BATCH, DEC_SEQ, D_MODEL), 1.0),
        'attn_norm_g': gain(ks[2], (N_A_LAYERS, D_MODEL)),
        'attn_w_qkv': nrm(ks[3], (N_A_LAYERS, D_MODEL, 3 * ATTN_WIDTH), D_MODEL ** -0.5),
        'attn_q_norm_g': gain(ks[4], (N_A_LAYERS, HEAD_DIM)),
        'attn_k_norm_g': gain(ks[5], (N_A_LAYERS, HEAD_DIM)),
        'attn_w_o': nrm(ks[6], (N_A_LAYERS, ATTN_WIDTH, D_MODEL), ATTN_WIDTH ** -0.5),
        'gmlp_norm_g': gain(ks[7], (N_B_LAYERS, D_MODEL)),
        'gmlp_w_in': nrm(ks[8], (N_B_LAYERS, D_MODEL, 2 * GATE_WIDTH), D_MODEL ** -0.5),
        'gmlp_v_norm_g': gain(ks[9], (N_B_LAYERS, GATE_WIDTH)),
        'gmlp_w_spatial': nrm(ks[10], (N_B_LAYERS, N_SPATIAL_GROUPS, CHUNK, CHUNK), CHUNK ** -0.5),
        'gmlp_b_spatial': gain(ks[11], (N_B_LAYERS, N_SPATIAL_GROUPS, CHUNK)),
        'gmlp_w_out': nrm(ks[12], (N_B_LAYERS, GATE_WIDTH, D_MODEL), GATE_WIDTH ** -0.5),
        'ffn_norm_g': gain(ks[13], (DEPTH, D_MODEL)),
        'ffn_w_up': nrm(ks[14], (DEPTH, D_MODEL, 2 * D_FF), D_MODEL ** -0.5),
        'ffn_conv_w': nrm(ks[15], (DEPTH, CONV_WIDTH, 2 * D_FF), CONV_WIDTH ** -0.5),
        'ffn_conv_b': nrm(ks[16], (DEPTH, 2 * D_FF), 0.02),
        'ffn_w_down': nrm(ks[17], (DEPTH, D_FF, D_MODEL), D_FF ** -0.5),
    }


def reference(x_prompt, x_sample, attn_norm_g, attn_w_qkv, attn_q_norm_g, attn_k_norm_g, attn_w_o,
              gmlp_norm_g, gmlp_w_in, gmlp_v_norm_g, gmlp_w_spatial, gmlp_b_spatial, gmlp_w_out,
              ffn_norm_g, ffn_w_up, ffn_conv_w, ffn_conv_b, ffn_w_down):
    y_prompt = trunk(x_prompt, attn_norm_g, attn_w_qkv, attn_q_norm_g, attn_k_norm_g, attn_w_o,
                     gmlp_norm_g, gmlp_w_in, gmlp_v_norm_g, gmlp_w_spatial, gmlp_b_spatial, gmlp_w_out,
                     ffn_norm_g, ffn_w_up, ffn_conv_w, ffn_conv_b, ffn_w_down)
    y_sample = trunk(x_sample, attn_norm_g, attn_w_qkv, attn_q_norm_g, attn_k_norm_g, attn_w_o,
                     gmlp_norm_g, gmlp_w_in, gmlp_v_norm_g, gmlp_w_spatial, gmlp_b_spatial, gmlp_w_out,
                     ffn_norm_g, ffn_w_up, ffn_conv_w, ffn_conv_b, ffn_w_down)
    return (y_prompt, y_sample)
```

```python
import functools

import numpy as np
import jax
import jax.numpy as jnp
from jax import lax
from jax.experimental import pallas as pl
from jax.experimental.pallas import tpu as pltpu

F32 = jnp.float32
BF16 = jnp.bfloat16

D_MODEL = 1024
SEQ = 4096
HEAD_DIM = 64
DILATIONS = (1, 4, 16)
HALF_WINDOW = 64
HEADS_PER_GROUP = 4
N_HEADS = len(DILATIONS) * HEADS_PER_GROUP
ATTN_WIDTH = N_HEADS * HEAD_DIM
GROUP_WIDTH = HEADS_PER_GROUP * HEAD_DIM
PAIR_WIDTH = 2 * HEAD_DIM
CHUNK = 128
N_SPATIAL_GROUPS = 8
D_FF = 2816
EPS = 1e-6
NEG_INF = -1e30

Q_BLOCK = 128
K_BLOCK = Q_BLOCK + 2 * HALF_WINDOW
FF_CHUNK = 256
N_FF_CHUNKS = D_FF // FF_CHUNK
HALO = 8
TM = 512
VMEM_LIMIT = 56 * 1024 * 1024


def _rms(x, g):
    ms = jnp.mean(x * x, axis=-1, keepdims=True)
    return x * lax.rsqrt(ms + EPS) * g


def _params(n_axes):
    return pltpu.CompilerParams(dimension_semantics=("arbitrary",) * n_axes,
                                vmem_limit_bytes=VMEM_LIMIT)


def _resident(shape):
    nd = len(shape)
    return pl.BlockSpec(shape, lambda *_: (0,) * nd, pipeline_mode=pl.Buffered(1))


def _qkv_kernel(x_ref, g_ref, w_ref, bd_ref, gv_ref, o_ref):
    h = _rms(x_ref[...], g_ref[...]).astype(BF16)
    slab = 3 * PAIR_WIDTH
    for j in range(ATTN_WIDTH * 3 // slab):
        y = jnp.dot(h, w_ref[:, j * slab:(j + 1) * slab], preferred_element_type=F32)
        qk = y[:, :2 * PAIR_WIDTH]
        ms = jnp.dot((qk * qk).astype(BF16), bd_ref[...], preferred_element_type=F32)
        qkn = qk * lax.rsqrt(ms + EPS) * gv_ref[...]
        o_ref[:, j * slab:j * slab + 2 * PAIR_WIDTH] = qkn.astype(BF16)
        o_ref[:, j * slab + 2 * PAIR_WIDTH:(j + 1) * slab] = y[:, 2 * PAIR_WIDTH:].astype(BF16)


def _qkv_proj(x2, g, w, bd, gv):
    n = x2.shape[0]
    return pl.pallas_call(
        _qkv_kernel,
        grid=(n // TM,),
        in_specs=[pl.BlockSpec((TM, D_MODEL), lambda i: (i, 0)),
                  _resident((1, D_MODEL)),
                  _resident((D_MODEL, 3 * ATTN_WIDTH)),
                  _resident((2 * PAIR_WIDTH, 2 * PAIR_WIDTH)),
                  _resident((1, 2 * PAIR_WIDTH))],
        out_specs=pl.BlockSpec((TM, 3 * ATTN_WIDTH), lambda i: (i, 0)),
        out_shape=jax.ShapeDtypeStruct((n, 3 * ATTN_WIDTH), BF16),
        compiler_params=_params(1),
        name="qkv_proj",
    )(x2, g, w, bd, gv)


def _alibi_slope(head):
    return float(np.exp2(-8.0 * (head + 1) / N_HEADS))


def _attn_kernel(qkv_ref, o_ref, lse_ref, bias_ref, *, sub_len, dilation, group):
    rho = pl.program_id(1)
    n_blocks = sub_len // Q_BLOCK
    offsets = (0, HALF_WINDOW, 2 * HALF_WINDOW)

    @pl.when((pl.program_id(0) == 0) & (rho == 0))
    def _build_bias():
        row = lax.broadcasted_iota(jnp.int32, (2 * Q_BLOCK, K_BLOCK), 0)
        key = lax.broadcasted_iota(jnp.int32, (2 * Q_BLOCK, K_BLOCK), 1)
        query = row & (Q_BLOCK - 1)
        for variant, off in enumerate(offsets):
            rel = jnp.abs(key - off - query)
            dist = (dilation * rel).astype(F32)
            for pair in range(2):
                head = group * HEADS_PER_GROUP + 2 * pair
                slope = jnp.where(row < Q_BLOCK, _alibi_slope(head), _alibi_slope(head + 1))
                bias_ref[variant, pair] = jnp.where(rel <= HALF_WINDOW, -(slope * dist), NEG_INF)

    @pl.when(rho == 0)
    def _init_lse():
        lse_ref[...] = jnp.zeros_like(lse_ref)

    lane = lax.broadcasted_iota(jnp.int32, (Q_BLOCK, PAIR_WIDTH), 1)
    first_head = lane < HEAD_DIM
    lse_lane = lax.broadcasted_iota(jnp.int32, (Q_BLOCK, lse_ref.shape[1]), 1)

    def block(q_start, k_start, variant):
        lse_rows = lse_ref[pl.ds(q_start, Q_BLOCK), :]
        for pair in range(2):
            base = pair * 3 * PAIR_WIDTH
            q = qkv_ref[pl.ds(q_start, Q_BLOCK), base:base + PAIR_WIDTH]
            k = qkv_ref[pl.ds(k_start, K_BLOCK), base + PAIR_WIDTH:base + 2 * PAIR_WIDTH]
            v = qkv_ref[pl.ds(k_start, K_BLOCK), base + 2 * PAIR_WIDTH:base + 3 * PAIR_WIDTH]
            zero = jnp.zeros_like(q)
            qq = jnp.concatenate([jnp.where(first_head, q, zero), jnp.where(first_head, zero, q)], axis=0)
            s = lax.dot_general(qq, k, (((1,), (1,)), ((), ())), preferred_element_type=F32)
            s = s + bias_ref[variant, pair]
            m = jnp.max(s, axis=1, keepdims=True)
            e = jnp.exp(s - m)
            l = jnp.sum(e, axis=1, keepdims=True)
            pv = jnp.dot(e.astype(BF16), v, preferred_element_type=F32) * (1.0 / l)
            o_pair = jnp.where(first_head, pv[:Q_BLOCK], pv[Q_BLOCK:])
            o_ref[pl.ds(q_start, Q_BLOCK), pair * PAIR_WIDTH:(pair + 1) * PAIR_WIDTH] = o_pair.astype(BF16)
            lse = m + jnp.log(l)
            col = rho * HEADS_PER_GROUP + 2 * pair
            lse_rows = jnp.where(lse_lane == col, lse[:Q_BLOCK], lse_rows)
            lse_rows = jnp.where(lse_lane == col + 1, lse[Q_BLOCK:], lse_rows)
        lse_ref[pl.ds(q_start, Q_BLOCK), :] = lse_rows

    block(0, 0, 0)

    def interior(n, carry):
        q_start = pl.multiple_of(n * Q_BLOCK, Q_BLOCK)
        k_start = pl.multiple_of(n * Q_BLOCK - HALF_WINDOW, HALF_WINDOW)
        block(q_start, k_start, 1)
        return carry

    lax.fori_loop(1, n_blocks - 1, interior, 0)
    block(sub_len - Q_BLOCK, sub_len - K_BLOCK, 2)


def _attention_group(qkv, batch, group):
    dilation = DILATIONS[group]
    sub_len = SEQ // dilation
    qkv_view = qkv.reshape(batch, sub_len, dilation * 3 * ATTN_WIDTH)
    kern = functools.partial(_attn_kernel, sub_len=sub_len, dilation=dilation, group=group)
    o, lse = pl.pallas_call(
        kern,
        grid=(batch, dilation),
        in_specs=[pl.BlockSpec((None, sub_len, 3 * GROUP_WIDTH),
                               lambda b, rho: (b, 0, rho * len(DILATIONS) + group))],
        out_specs=[pl.BlockSpec((None, sub_len, GROUP_WIDTH), lambda b, rho: (b, 0, rho)),
                   pl.BlockSpec((None, sub_len, dilation * HEADS_PER_GROUP), lambda b, rho: (b, 0, 0))],
        out_shape=[jax.ShapeDtypeStruct((batch, sub_len, dilation * GROUP_WIDTH), BF16),
                   jax.ShapeDtypeStruct((batch, sub_len, dilation * HEADS_PER_GROUP), F32)],
        scratch_shapes=[pltpu.VMEM((3, 2, 2 * Q_BLOCK, K_BLOCK), F32)],
        compiler_params=_params(2),
        name=f"dilated_attn_g{group}",
    )(qkv_view)
    return o.reshape(batch * SEQ, GROUP_WIDTH), lse.reshape(batch * SEQ, HEADS_PER_GROUP)


def _attn_out_kernel(x_ref, o0_ref, o1_ref, o2_ref, l0_ref, l1_ref, l2_ref, w_ref, out_ref):
    lses = [l0_ref[...], l1_ref[...], l2_ref[...]]
    top = jnp.maximum(jnp.maximum(lses[0], lses[1]), lses[2])
    es = [jnp.exp(l - top) for l in lses]
    inv = 1.0 / (es[0] + es[1] + es[2])
    lane = lax.broadcasted_iota(jnp.int32, (x_ref.shape[0], GROUP_WIDTH), 1)
    acc = x_ref[...]
    for g, (o_ref, e) in enumerate(zip((o0_ref, o1_ref, o2_ref), es)):
        alpha = e * inv
        wide = jnp.where(lane < HEAD_DIM, alpha[:, 0:1],
                         jnp.where(lane < 2 * HEAD_DIM, alpha[:, 1:2],
                                   jnp.where(lane < 3 * HEAD_DIM, alpha[:, 2:3], alpha[:, 3:4])))
        scaled = (o_ref[...].astype(F32) * wide).astype(BF16)
        acc = acc + jnp.dot(scaled, w_ref[g * GROUP_WIDTH:(g + 1) * GROUP_WIDTH, :],
                            preferred_element_type=F32)
    out_ref[...] = acc


def _attn_out(x2, os, lses, w_o):
    n = x2.shape[0]
    row = lambda width: pl.BlockSpec((TM, width), lambda i: (i, 0))
    return pl.pallas_call(
        _attn_out_kernel,
        grid=(n // TM,),
        in_specs=[row(D_MODEL)] + [row(GROUP_WIDTH)] * 3 + [row(HEADS_PER_GROUP)] * 3
                 + [_resident((ATTN_WIDTH, D_MODEL))],
        out_specs=row(D_MODEL),
        out_shape=jax.ShapeDtypeStruct((n, D_MODEL), F32),
        compiler_params=_params(1),
        name="attn_out_proj",
    )(x2, *os, *lses, w_o)


def _gmlp_kernel(x_ref, g_ref, win_ref, gv_ref, ws_ref, bs_ref, wout_ref, out_ref, gated_ref):
    x = x_ref[...]
    h = _rms(x, g_ref[...]).astype(BF16)
    z = jax.nn.gelu(jnp.dot(h, win_ref[...], preferred_element_type=F32))
    u = z[:, :D_MODEL]
    v = _rms(z[:, D_MODEL:], gv_ref[...]).astype(BF16)
    for n in range(x.shape[0] // CHUNK):
        rows = slice(n * CHUNK, (n + 1) * CHUNK)
        for sg in range(N_SPATIAL_GROUPS):
            cols = slice(sg * CHUNK, (sg + 1) * CHUNK)
            s = jnp.dot(ws_ref[sg], v[rows, cols], preferred_element_type=F32) + bs_ref[sg]
            gated_ref[rows, cols] = (u[rows, cols] * s).astype(BF16)
    out_ref[...] = x + jnp.dot(gated_ref[...], wout_ref[...], preferred_element_type=F32)


def _gmlp(x2, g, w_in, g_v, w_s, b_s, w_out):
    n = x2.shape[0]
    return pl.pallas_call(
        _gmlp_kernel,
        grid=(n // TM,),
        in_specs=[pl.BlockSpec((TM, D_MODEL), lambda i: (i, 0)),
                  _resident((1, D_MODEL)),
                  _resident((D_MODEL, 2 * D_MODEL)),
                  _resident((1, D_MODEL)),
                  _resident((N_SPATIAL_GROUPS, CHUNK, CHUNK)),
                  _resident((N_SPATIAL_GROUPS, CHUNK, CHUNK)),
                  _resident((D_MODEL, D_MODEL))],
        out_specs=pl.BlockSpec((TM, D_MODEL), lambda i: (i, 0)),
        out_shape=jax.ShapeDtypeStruct((n, D_MODEL), F32),
        scratch_shapes=[pltpu.VMEM((TM, D_MODEL), BF16)],
        compiler_params=_params(1),
        name="chunk_gmlp",
    )(x2, g, w_in, g_v, w_s, b_s, w_out)


def _ffn_kernel(xm_ref, xp_ref, xn_ref, g_ref, wup_ref, cw_ref, cb_ref, wdn_ref, out_ref, h_ref, a_ref):
    i = pl.program_id(1)
    tm = xm_ref.shape[0]
    g = g_ref[...]
    xm = xm_ref[...]
    xp = jnp.where(i > 0, xp_ref[...], 0.0)
    xn = jnp.where(i < pl.num_programs(1) - 1, xn_ref[...], 0.0)
    h_ref[...] = jnp.concatenate([_rms(xp, g), _rms(xm, g), _rms(xn, g)], axis=0).astype(BF16)
    out_ref[...] = xm

    def chunk(c, carry):
        a_ref[...] = jnp.dot(h_ref[...], wup_ref[c], preferred_element_type=F32)
        cw = cw_ref[c]
        a = (cw[0:1] * a_ref[HALO - 1:HALO - 1 + tm, :] + cw[1:2] * a_ref[HALO:HALO + tm, :]
             + cw[2:3] * a_ref[HALO + 1:HALO + 1 + tm, :] + cb_ref[c])
        gate = a[:, :FF_CHUNK]
        val = a[:, FF_CHUNK:]
        act = gate * pl.reciprocal(1.0 + jnp.exp(-gate), approx=True) * val
        out_ref[...] += jnp.dot(act.astype(BF16), wdn_ref[c], preferred_element_type=F32)
        return carry

    lax.fori_loop(0, N_FF_CHUNKS, chunk, 0)


def _ffn(x3, g, w_up, conv_w, conv_b, w_down):
    batch = x3.shape[0]
    tiles = SEQ // TM
    halo_blocks = TM // HALO
    last_halo = SEQ // HALO - 1
    return pl.pallas_call(
        _ffn_kernel,
        grid=(batch, tiles),
        in_specs=[pl.BlockSpec((None, TM, D_MODEL), lambda b, i: (b, i, 0)),
                  pl.BlockSpec((None, HALO, D_MODEL),
                               lambda b, i: (b, jnp.maximum(i * halo_blocks - 1, 0), 0)),
                  pl.BlockSpec((None, HALO, D_MODEL),
                               lambda b, i: (b, jnp.minimum((i + 1) * halo_blocks, last_halo), 0)),
                  _resident((1, D_MODEL)),
                  _resident((N_FF_CHUNKS, D_MODEL, 2 * FF_CHUNK)),
                  _resident((N_FF_CHUNKS, 3, 2 * FF_CHUNK)),
                  _resident((N_FF_CHUNKS, 1, 2 * FF_CHUNK)),
                  _resident((N_FF_CHUNKS, FF_CHUNK, D_MODEL))],
        out_specs=pl.BlockSpec((None, TM, D_MODEL), lambda b, i: (b, i, 0)),
        out_shape=jax.ShapeDtypeStruct(x3.shape, F32),
        scratch_shapes=[pltpu.VMEM((TM + 2 * HALO, D_MODEL), BF16),
                        pltpu.VMEM((TM + 2 * HALO, 2 * FF_CHUNK), F32)],
        compiler_params=_params(2),
        name="conv_gated_ffn",
    )(x3, x3, x3, g, w_up, conv_w, conv_b, w_down)


def _qkv_column_order():
    cols = []
    for group in range(len(DILATIONS)):
        for pair in range(2):
            for part in range(3):
                for h2 in range(2):
                    head = group * HEADS_PER_GROUP + 2 * pair + h2
                    start = part * ATTN_WIDTH + head * HEAD_DIM
                    cols.extend(range(start, start + HEAD_DIM))
    return np.asarray(cols, dtype=np.int32)


def _chunk_gate_val(t):
    lead = t.shape[:-1]
    gv = t.reshape(lead + (2, N_FF_CHUNKS, FF_CHUNK))
    gv = jnp.moveaxis(gv, -2, 0)
    return gv.reshape((N_FF_CHUNKS,) + lead + (2 * FF_CHUNK,))


def _prep_ffn(layer, ffn_norm_g, ffn_w_up, ffn_conv_w, ffn_conv_b, ffn_w_down):
    return (ffn_norm_g[layer][None, :],
            _chunk_gate_val(ffn_w_up[layer]).astype(BF16),
            _chunk_gate_val(ffn_conv_w[layer]),
            _chunk_gate_val(ffn_conv_b[layer][None, :]),
            ffn_w_down[layer].reshape(N_FF_CHUNKS, FF_CHUNK, D_MODEL).astype(BF16))


def _trunk(x, attn_w, gmlp_w, ffn_w0, ffn_w1):
    batch = x.shape[0]
    n = batch * SEQ
    x2 = x.reshape(n, D_MODEL)

    norm_g, w_qkv, block_avg, qk_gain, w_o = attn_w
    qkv = _qkv_proj(x2, norm_g, w_qkv, block_avg, qk_gain)
    outs = [_attention_group(qkv, batch, group) for group in range(len(DILATIONS))]
    x2 = _attn_out(x2, [o for o, _ in outs], [l for _, l in outs], w_o)
    x3 = _ffn(x2.reshape(batch, SEQ, D_MODEL), *ffn_w0)

    x2 = _gmlp(x3.reshape(n, D_MODEL), *gmlp_w)
    return _ffn(x2.reshape(batch, SEQ, D_MODEL), *ffn_w1)


def kernel(x_prompt, x_sample, attn_norm_g, attn_w_qkv, attn_q_norm_g, attn_k_norm_g, attn_w_o, gmlp_norm_g, gmlp_w_in, gmlp_v_norm_g, gmlp_w_spatial, gmlp_b_spatial, gmlp_w_out, ffn_norm_g, ffn_w_up, ffn_conv_w, ffn_conv_b, ffn_w_down):
    head_of = np.arange(2 * PAIR_WIDTH) // HEAD_DIM
    block_avg = jnp.asarray((head_of[:, None] == head_of[None, :]) / HEAD_DIM, dtype=BF16)
    qk_gain = jnp.concatenate([jnp.tile(attn_q_norm_g[0] * HEAD_DIM ** -0.5, 2),
                               jnp.tile(attn_k_norm_g[0], 2)])[None, :]
    attn_w = (attn_norm_g[0][None, :],
              attn_w_qkv[0][:, _qkv_column_order()].astype(BF16),
              block_avg, qk_gain,
              attn_w_o[0].astype(BF16))
    gmlp_w = (gmlp_norm_g[0][None, :],
              gmlp_w_in[0].astype(BF16),
              gmlp_v_norm_g[0][None, :],
              gmlp_w_spatial[0].astype(BF16),
              jnp.broadcast_to(gmlp_b_spatial[0][:, :, None], (N_SPATIAL_GROUPS, CHUNK, CHUNK)),
              gmlp_w_out[0].astype(BF16))
    ffn_w0 = _prep_ffn(0, ffn_norm_g, ffn_w_up, ffn_conv_w, ffn_conv_b, ffn_w_down)
    ffn_w1 = _prep_ffn(1, ffn_norm_g, ffn_w_up, ffn_conv_w, ffn_conv_b, ffn_w_down)
    return (_trunk(x_prompt, attn_w, gmlp_w, ffn_w0, ffn_w1),
            _trunk(x_sample, attn_w, gmlp_w, ffn_w0, ffn_w1))
```

```python
import functools

import numpy as np
import jax
import jax.numpy as jnp
from jax import lax
from jax.experimental import pallas as pl
from jax.experimental.pallas import tpu as pltpu

F32 = jnp.float32
BF16 = jnp.bfloat16

D_MODEL = 1024
SEQ = 4096
HEAD_DIM = 64
DILATIONS = (1, 4, 16)
HALF_WINDOW = 64
HEADS_PER_GROUP = 4
N_HEADS = len(DILATIONS) * HEADS_PER_GROUP
ATTN_WIDTH = N_HEADS * HEAD_DIM
GROUP_WIDTH = HEADS_PER_GROUP * HEAD_DIM
PAIR_WIDTH = 2 * HEAD_DIM
CHUNK = 128
N_SPATIAL_GROUPS = 8
D_FF = 2816
EPS = 1e-6
NEG_INF = -1e30

Q_BLOCK = 128
K_BLOCK = Q_BLOCK + 2 * HALF_WINDOW
FF_CHUNK = 256
N_FF_CHUNKS = D_FF // FF_CHUNK
HALO = 8
TM = 512
VMEM_LIMIT = 56 * 1024 * 1024


def _rms(x, g):
    ms = jnp.mean(x * x, axis=-1, keepdims=True)
    return x * lax.rsqrt(ms + EPS) * g


def _params(n_axes):
    return pltpu.CompilerParams(dimension_semantics=("arbitrary",) * n_axes,
                                vmem_limit_bytes=VMEM_LIMIT)


def _resident(shape):
    nd = len(shape)
    return pl.BlockSpec(shape, lambda *_: (0,) * nd, pipeline_mode=pl.Buffered(1))


def _qkv_kernel(x_ref, g_ref, w_ref, bd_ref, gv_ref, o_ref):
    h = _rms(x_ref[...], g_ref[...]).astype(BF16)
    slab = 3 * PAIR_WIDTH
    for j in range(ATTN_WIDTH * 3 // slab):
        y = jnp.dot(h, w_ref[:, j * slab:(j + 1) * slab], preferred_element_type=F32)
        qk = y[:, :2 * PAIR_WIDTH]
        ms = jnp.dot((qk * qk).astype(BF16), bd_ref[...], preferred_element_type=F32)
        qkn = qk * lax.rsqrt(ms + EPS) * gv_ref[...]
        o_ref[:, j * slab:j * slab + 2 * PAIR_WIDTH] = qkn.astype(BF16)
        o_ref[:, j * slab + 2 * PAIR_WIDTH:(j + 1) * slab] = y[:, 2 * PAIR_WIDTH:].astype(BF16)


def _qkv_proj(x2, g, w, bd, gv):
    n = x2.shape[0]
    return pl.pallas_call(
        _qkv_kernel,
        grid=(n // TM,),
        in_specs=[pl.BlockSpec((TM, D_MODEL), lambda i: (i, 0)),
                  _resident((1, D_MODEL)),
                  _resident((D_MODEL, 3 * ATTN_WIDTH)),
                  _resident((2 * PAIR_WIDTH, 2 * PAIR_WIDTH)),
                  _resident((1, 2 * PAIR_WIDTH))],
        out_specs=pl.BlockSpec((TM, 3 * ATTN_WIDTH), lambda i: (i, 0)),
        out_shape=jax.ShapeDtypeStruct((n, 3 * ATTN_WIDTH), BF16),
        compiler_params=_params(1),
        name="qkv_proj",
    )(x2, g, w, bd, gv)


def _alibi_slope(head):
    return float(np.exp2(-8.0 * (head + 1) / N_HEADS))


def _attn_kernel(qkv_ref, o_ref, lse_ref, bias_ref, *, sub_len, dilation, group):
    rho = pl.program_id(1)
    n_blocks = sub_len // Q_BLOCK
    offsets = (0, HALF_WINDOW, 2 * HALF_WINDOW)

    @pl.when((pl.program_id(0) == 0) & (rho == 0))
    def _build_bias():
        row = lax.broadcasted_iota(jnp.int32, (2 * Q_BLOCK, K_BLOCK), 0)
        key = lax.broadcasted_iota(jnp.int32, (2 * Q_BLOCK, K_BLOCK), 1)
        query = row & (Q_BLOCK - 1)
        for variant, off in enumerate(offsets):
            rel = jnp.abs(key - off - query)
            dist = (dilation * rel).astype(F32)
            for pair in range(2):
                head = group * HEADS_PER_GROUP + 2 * pair
                slope = jnp.where(row < Q_BLOCK, _alibi_slope(head), _alibi_slope(head + 1))
                bias_ref[variant, pair] = jnp.where(rel <= HALF_WINDOW, -(slope * dist), NEG_INF)

    @pl.when(rho == 0)
    def _init_lse():
        lse_ref[...] = jnp.zeros_like(lse_ref)

    lane = lax.broadcasted_iota(jnp.int32, (Q_BLOCK, PAIR_WIDTH), 1)
    first_head = lane < HEAD_DIM
    lse_lane = lax.broadcasted_iota(jnp.int32, (Q_BLOCK, lse_ref.shape[1]), 1)

    def block(q_start, k_start, variant):
        lse_rows = lse_ref[pl.ds(q_start, Q_BLOCK), :]
        for pair in range(2):
            base = pair * 3 * PAIR_WIDTH
            q = qkv_ref[pl.ds(q_start, Q_BLOCK), base:base + PAIR_WIDTH]
            k = qkv_ref[pl.ds(k_start, K_BLOCK), base + PAIR_WIDTH:base + 2 * PAIR_WIDTH]
            v = qkv_ref[pl.ds(k_start, K_BLOCK), base + 2 * PAIR_WIDTH:base + 3 * PAIR_WIDTH]
            zero = jnp.zeros_like(q)
            qq = jnp.concatenate([jnp.where(first_head, q, zero), jnp.where(first_head, zero, q)], axis=0)
            s = lax.dot_general(qq, k, (((1,), (1,)), ((), ())), preferred_element_type=F32)
            s = s + bias_ref[variant, pair]
            m = jnp.max(s, axis=1, keepdims=True)
            e = jnp.exp(s - m)
            l = jnp.sum(e, axis=1, keepdims=True)
            pv = jnp.dot(e.astype(BF16), v, preferred_element_type=F32) * (1.0 / l)
            o_pair = jnp.where(first_head, pv[:Q_BLOCK], pv[Q_BLOCK:])
            o_ref[pl.ds(q_start, Q_BLOCK), pair * PAIR_WIDTH:(pair + 1) * PAIR_WIDTH] = o_pair.astype(BF16)
            lse = m + jnp.log(l)
            col = rho * HEADS_PER_GROUP + 2 * pair
            lse_rows = jnp.where(lse_lane == col, lse[:Q_BLOCK], lse_rows)
            lse_rows = jnp.where(lse_lane == col + 1, lse[Q_BLOCK:], lse_rows)
        lse_ref[pl.ds(q_start, Q_BLOCK), :] = lse_rows

    block(0, 0, 0)

    def interior(n, carry):
        q_start = pl.multiple_of(n * Q_BLOCK, Q_BLOCK)
        k_start = pl.multiple_of(n * Q_BLOCK - HALF_WINDOW, HALF_WINDOW)
        block(q_start, k_start, 1)
        return carry

    lax.fori_loop(1, n_blocks - 1, interior, 0)
    block(sub_len - Q_BLOCK, sub_len - K_BLOCK, 2)


def _attention_group(qkv, batch, group):
    dilation = DILATIONS[group]
    sub_len = SEQ // dilation
    qkv_view = qkv.reshape(batch, sub_len, dilation * 3 * ATTN_WIDTH)
    kern = functools.partial(_attn_kernel, sub_len=sub_len, dilation=dilation, group=group)
    o, lse = pl.pallas_call(
        kern,
        grid=(batch, dilation),
        in_specs=[pl.BlockSpec((None, sub_len, 3 * GROUP_WIDTH),
                               lambda b, rho: (b, 0, rho * len(DILATIONS) + group))],
        out_specs=[pl.BlockSpec((None, sub_len, GROUP_WIDTH), lambda b, rho: (b, 0, rho)),
                   pl.BlockSpec((None, sub_len, dilation * HEADS_PER_GROUP), lambda b, rho: (b, 0, 0))],
        out_shape=[jax.ShapeDtypeStruct((batch, sub_len, dilation * GROUP_WIDTH), BF16),
                   jax.ShapeDtypeStruct((batch, sub_len, dilation * HEADS_PER_GROUP), F32)],
        scratch_shapes=[pltpu.VMEM((3, 2, 2 * Q_BLOCK, K_BLOCK), F32)],
        compiler_params=_params(2),
        name=f"dilated_attn_g{group}",
    )(qkv_view)
    return o.reshape(batch * SEQ, GROUP_WIDTH), lse.reshape(batch * SEQ, HEADS_PER_GROUP)


def _attn_out_kernel(x_ref, o0_ref, o1_ref, o2_ref, l0_ref, l1_ref, l2_ref, w_ref, out_ref):
    lses = [l0_ref[...], l1_ref[...], l2_ref[...]]
    top = jnp.maximum(jnp.maximum(lses[0], lses[1]), lses[2])
    es = [jnp.exp(l - top) for l in lses]
    inv = 1.0 / (es[0] + es[1] + es[2])
    lane = lax.broadcasted_iota(jnp.int32, (x_ref.shape[0], GROUP_WIDTH), 1)
    acc = x_ref[...]
    for g, (o_ref, e) in enumerate(zip((o0_ref, o1_ref, o2_ref), es)):
        alpha = e * inv
        wide = jnp.where(lane < HEAD_DIM, alpha[:, 0:1],
                         jnp.where(lane < 2 * HEAD_DIM, alpha[:, 1:2],
                                   jnp.where(lane < 3 * HEAD_DIM, alpha[:, 2:3], alpha[:, 3:4])))
        scaled = (o_ref[...].astype(F32) * wide).astype(BF16)
        acc = acc + jnp.dot(scaled, w_ref[g * GROUP_WIDTH:(g + 1) * GROUP_WIDTH, :],
                            preferred_element_type=F32)
    out_ref[...] = acc


def _attn_out(x2, os, lses, w_o):
    n = x2.shape[0]
    row = lambda width: pl.BlockSpec((TM, width), lambda i: (i, 0))
    return pl.pallas_call(
        _attn_out_kernel,
        grid=(n // TM,),
        in_specs=[row(D_MODEL)] + [row(GROUP_WIDTH)] * 3 + [row(HEADS_PER_GROUP)] * 3
                 + [_resident((ATTN_WIDTH, D_MODEL))],
        out_specs=row(D_MODEL),
        out_shape=jax.ShapeDtypeStruct((n, D_MODEL), F32),
        compiler_params=_params(1),
        name="attn_out_proj",
    )(x2, *os, *lses, w_o)


def _gmlp_kernel(x_ref, g_ref, win_ref, gv_ref, ws_ref, bs_ref, wout_ref, out_ref, gated_ref):
    x = x_ref[...]
    h = _rms(x, g_ref[...]).astype(BF16)
    z = jax.nn.gelu(jnp.dot(h, win_ref[...], preferred_element_type=F32))
    u = z[:, :D_MODEL]
    v = _rms(z[:, D_MODEL:], gv_ref[...]).astype(BF16)
    for n in range(x.shape[0] // CHUNK):
        rows = slice(n * CHUNK, (n + 1) * CHUNK)
        for sg in range(N_SPATIAL_GROUPS):
            cols = slice(sg * CHUNK, (sg + 1) * CHUNK)
            s = jnp.dot(ws_ref[sg], v[rows, cols], preferred_element_type=F32) + bs_ref[sg]
            gated_ref[rows, cols] = (u[rows, cols] * s).astype(BF16)
    out_ref[...] = x + jnp.dot(gated_ref[...], wout_ref[...], preferred_element_type=F32)


def _gmlp(x2, g, w_in, g_v, w_s, b_s, w_out):
    n = x2.shape[0]
    return pl.pallas_call(
        _gmlp_kernel,
        grid=(n // TM,),
        in_specs=[pl.BlockSpec((TM, D_MODEL), lambda i: (i, 0)),
                  _resident((1, D_MODEL)),
                  _resident((D_MODEL, 2 * D_MODEL)),
                  _resident((1, D_MODEL)),
                  _resident((N_SPATIAL_GROUPS, CHUNK, CHUNK)),
                  _resident((N_SPATIAL_GROUPS, CHUNK, CHUNK)),
                  _resident((D_MODEL, D_MODEL))],
        out_specs=pl.BlockSpec((TM, D_MODEL), lambda i: (i, 0)),
        out_shape=jax.ShapeDtypeStruct((n, D_MODEL), F32),
        scratch_shapes=[pltpu.VMEM((TM, D_MODEL), BF16)],
        compiler_params=_params(1),
        name="chunk_gmlp",
    )(x2, g, w_in, g_v, w_s, b_s, w_out)


def _ffn_kernel(xm_ref, xp_ref, xn_ref, g_ref, wup_ref, cw_ref, cb_ref, wdn_ref, out_ref, h_ref, a0_ref, a1_ref):
    i = pl.program_id(1)
    tm = xm_ref.shape[0]
    g = g_ref[...]
    xm = xm_ref[...]
    xp = jnp.where(i > 0, xp_ref[...], 0.0)
    xn = jnp.where(i < pl.num_programs(1) - 1, xn_ref[...], 0.0)
    h_ref[...] = jnp.concatenate([_rms(xp, g), _rms(xm, g), _rms(xn, g)], axis=0).astype(BF16)
    out_ref[...] = xm

    def up(c, a_ref):
        a_ref[...] = jnp.dot(h_ref[...], wup_ref[c], preferred_element_type=F32)

    def down(c, a_ref):
        cw = cw_ref[c]
        a = (cw[0:1] * a_ref[HALO - 1:HALO - 1 + tm, :] + cw[1:2] * a_ref[HALO:HALO + tm, :]
             + cw[2:3] * a_ref[HALO + 1:HALO + 1 + tm, :] + cb_ref[c])
        gate = a[:, :FF_CHUNK]
        val = a[:, FF_CHUNK:]
        act = gate * pl.reciprocal(1.0 + jnp.exp(-gate), approx=True) * val
        out_ref[...] += jnp.dot(act.astype(BF16), wdn_ref[c], preferred_element_type=F32)

    up(0, a0_ref)

    def two_chunks(k, carry):
        c = 2 * k
        up(c + 1, a1_ref)
        down(c, a0_ref)
        up(c + 2, a0_ref)
        down(c + 1, a1_ref)
        return carry

    assert N_FF_CHUNKS % 2 == 1
    lax.fori_loop(0, N_FF_CHUNKS // 2, two_chunks, 0)
    down(N_FF_CHUNKS - 1, a0_ref)


def _ffn(x3, g, w_up, conv_w, conv_b, w_down):
    batch = x3.shape[0]
    tiles = SEQ // TM
    halo_blocks = TM // HALO
    last_halo = SEQ // HALO - 1
    return pl.pallas_call(
        _ffn_kernel,
        grid=(batch, tiles),
        in_specs=[pl.BlockSpec((None, TM, D_MODEL), lambda b, i: (b, i, 0)),
                  pl.BlockSpec((None, HALO, D_MODEL),
                               lambda b, i: (b, jnp.maximum(i * halo_blocks - 1, 0), 0)),
                  pl.BlockSpec((None, HALO, D_MODEL),
                               lambda b, i: (b, jnp.minimum((i + 1) * halo_blocks, last_halo), 0)),
                  _resident((1, D_MODEL)),
                  _resident((N_FF_CHUNKS, D_MODEL, 2 * FF_CHUNK)),
                  _resident((N_FF_CHUNKS, 3, 2 * FF_CHUNK)),
                  _resident((N_FF_CHUNKS, 1, 2 * FF_CHUNK)),
                  _resident((N_FF_CHUNKS, FF_CHUNK, D_MODEL))],
        out_specs=pl.BlockSpec((None, TM, D_MODEL), lambda b, i: (b, i, 0)),
        out_shape=jax.ShapeDtypeStruct(x3.shape, F32),
        scratch_shapes=[pltpu.VMEM((TM + 2 * HALO, D_MODEL), BF16),
                        pltpu.VMEM((TM + 2 * HALO, 2 * FF_CHUNK), F32),
                        pltpu.VMEM((TM + 2 * HALO, 2 * FF_CHUNK), F32)],
        compiler_params=_params(2),
        name="conv_gated_ffn",
    )(x3, x3, x3, g, w_up, conv_w, conv_b, w_down)


def _qkv_column_order():
    cols = []
    for group in range(len(DILATIONS)):
        for pair in range(2):
            for part in range(3):
                for h2 in range(2):
                    head = group * HEADS_PER_GROUP + 2 * pair + h2
                    start = part * ATTN_WIDTH + head * HEAD_DIM
                    cols.extend(range(start, start + HEAD_DIM))
    return np.asarray(cols, dtype=np.int32)


def _chunk_gate_val(t):
    lead = t.shape[:-1]
    gv = t.reshape(lead + (2, N_FF_CHUNKS, FF_CHUNK))
    gv = jnp.moveaxis(gv, -2, 0)
    return gv.reshape((N_FF_CHUNKS,) + lead + (2 * FF_CHUNK,))


def _prep_ffn(layer, ffn_norm_g, ffn_w_up, ffn_conv_w, ffn_conv_b, ffn_w_down):
    return (ffn_norm_g[layer][None, :],
            _chunk_gate_val(ffn_w_up[layer]).astype(BF16),
            _chunk_gate_val(ffn_conv_w[layer]),
            _chunk_gate_val(ffn_conv_b[layer][None, :]),
            ffn_w_down[layer].reshape(N_FF_CHUNKS, FF_CHUNK, D_MODEL).astype(BF16))


def _trunk(x, attn_w, gmlp_w, ffn_w0, ffn_w1):
    batch = x.shape[0]
    n = batch * SEQ
    x2 = x.reshape(n, D_MODEL)

    norm_g, w_qkv, block_avg, qk_gain, w_o = attn_w
    qkv = _qkv_proj(x2, norm_g, w_qkv, block_avg, qk_gain)
    outs = [_attention_group(qkv, batch, group) for group in range(len(DILATIONS))]
    x2 = _attn_out(x2, [o for o, _ in outs], [l for _, l in outs], w_o)
    x3 = _ffn(x2.reshape(batch, SEQ, D_MODEL), *ffn_w0)

    x2 = _gmlp(x3.reshape(n, D_MODEL), *gmlp_w)
    return _ffn(x2.reshape(batch, SEQ, D_MODEL), *ffn_w1)


def kernel(x_prompt, x_sample, attn_norm_g, attn_w_qkv, attn_q_norm_g, attn_k_norm_g, attn_w_o, gmlp_norm_g, gmlp_w_in, gmlp_v_norm_g, gmlp_w_spatial, gmlp_b_spatial, gmlp_w_out, ffn_norm_g, ffn_w_up, ffn_conv_w, ffn_conv_b, ffn_w_down):
    head_of = np.arange(2 * PAIR_WIDTH) // HEAD_DIM
    block_avg = jnp.asarray((head_of[:, None] == head_of[None, :]) / HEAD_DIM, dtype=BF16)
    qk_gain = jnp.concatenate([jnp.tile(attn_q_norm_g[0] * HEAD_DIM ** -0.5, 2),
                               jnp.tile(attn_k_norm_g[0], 2)])[None, :]
    attn_w = (attn_norm_g[0][None, :],
              attn_w_qkv[0][:, _qkv_column_order()].astype(BF16),
              block_avg, qk_gain,
              attn_w_o[0].astype(BF16))
    gmlp_w = (gmlp_norm_g[0][None, :],
              gmlp_w_in[0].astype(BF16),
              gmlp_v_norm_g[0][None, :],
              gmlp_w_spatial[0].astype(BF16),
              jnp.broadcast_to(gmlp_b_spatial[0][:, :, None], (N_SPATIAL_GROUPS, CHUNK, CHUNK)),
              gmlp_w_out[0].astype(BF16))
    ffn_w0 = _prep_ffn(0, ffn_norm_g, ffn_w_up, ffn_conv_w, ffn_conv_b, ffn_w_down)
    ffn_w1 = _prep_ffn(1, ffn_norm_g, ffn_w_up, ffn_conv_w, ffn_conv_b, ffn_w_down)
    return (_trunk(x_prompt, attn_w, gmlp_w, ffn_w0, ffn_w1),
            _trunk(x_sample, attn_w, gmlp_w, ffn_w0, ffn_w1))
```

```python
import functools

import numpy as np
import jax
import jax.numpy as jnp
from jax import lax
from jax.experimental import pallas as pl
from jax.experimental.pallas import tpu as pltpu

F32 = jnp.float32
BF16 = jnp.bfloat16

D_MODEL = 1024
SEQ = 4096
HEAD_DIM = 64
DILATIONS = (1, 4, 16)
N_GROUPS = len(DILATIONS)
HALF_WINDOW = 64
HEADS_PER_GROUP = 4
N_HEADS = N_GROUPS * HEADS_PER_GROUP
ATTN_WIDTH = N_HEADS * HEAD_DIM
GROUP_WIDTH = HEADS_PER_GROUP * HEAD_DIM
LANES = 128
PAIR_WIDTH = 2 * HEAD_DIM
assert PAIR_WIDTH == LANES
QKV_GROUP_WIDTH = 3 * GROUP_WIDTH
O_GROUP_WIDTH = GROUP_WIDTH + LANES
CHUNK = 128
N_SPATIAL_GROUPS = 8
D_FF = 2816
EPS = 1e-6
NEG_INF = -1e30

Q_BLOCK = 128
K_BLOCK = Q_BLOCK + 2 * HALF_WINDOW
ATTN_UNROLL = 4
FF_CHUNK = 256
N_FF_CHUNKS = D_FF // FF_CHUNK
HALO = 8
TM = 512
VMEM_LIMIT = 56 * 1024 * 1024


def _rms(x, g):
    ms = jnp.mean(x * x, axis=-1, keepdims=True)
    return x * lax.rsqrt(ms + EPS) * g


def _params(n_axes):
    return pltpu.CompilerParams(dimension_semantics=("arbitrary",) * n_axes,
                                vmem_limit_bytes=VMEM_LIMIT)


def _resident(shape):
    nd = len(shape)
    return pl.BlockSpec(shape, lambda *_: (0,) * nd, pipeline_mode=pl.Buffered(1))


def _residue_major_spec(dilation, rows, width):
    return pl.BlockSpec((None, dilation, rows // dilation, width), lambda b, i: (b, 0, i, 0))


def _qkv_kernel(x_ref, g_ref, w_ref, bd_ref, gv_ref, o0_ref, o1_ref, o2_ref, y_ref):
    tm = x_ref.shape[0]
    h = _rms(x_ref[...], g_ref[...]).astype(BF16)
    out_refs = (o0_ref, o1_ref, o2_ref)
    for group, dilation in enumerate(DILATIONS):
        for pair in range(2):
            col0 = group * QKV_GROUP_WIDTH + pair * 3 * LANES
            y = jnp.dot(h, w_ref[:, col0:col0 + 3 * LANES], preferred_element_type=F32)
            qk = y[:, :2 * LANES]
            ms = jnp.dot((qk * qk).astype(BF16), bd_ref[...], preferred_element_type=F32)
            qkn = qk * lax.rsqrt(ms + EPS) * gv_ref[...]
            slabs = (qkn[:, :LANES], qkn[:, LANES:], y[:, 2 * LANES:])
            out_ref = out_refs[group]
            for s, slab in enumerate(slabs):
                cols = slice((pair * 3 + s) * LANES, (pair * 3 + s + 1) * LANES)
                if dilation == 1:
                    out_ref[0, :, cols] = slab.astype(BF16)
                else:
                    slot = ((group - 1) * 2 + pair) * 3 + s
                    y_ref[slot] = slab
                    for rho in range(dilation):
                        rows = y_ref[slot, pl.ds(rho, tm // dilation, stride=dilation), :]
                        out_ref[rho, :, cols] = rows.astype(BF16)


def _qkv_proj(x3, g, w, bd, gv):
    batch = x3.shape[0]
    n_strided_slabs = (N_GROUPS - 1) * 2 * 3
    return pl.pallas_call(
        _qkv_kernel,
        grid=(batch, SEQ // TM),
        in_specs=[pl.BlockSpec((None, TM, D_MODEL), lambda b, i: (b, i, 0)),
                  _resident((1, D_MODEL)),
                  _resident((D_MODEL, 3 * ATTN_WIDTH)),
                  _resident((2 * LANES, 2 * LANES)),
                  _resident((1, 2 * LANES))],
        out_specs=[_residue_major_spec(r, TM, QKV_GROUP_WIDTH) for r in DILATIONS],
        out_shape=[jax.ShapeDtypeStruct((batch, r, SEQ // r, QKV_GROUP_WIDTH), BF16) for r in DILATIONS],
        scratch_shapes=[pltpu.VMEM((n_strided_slabs, TM, LANES), F32)],
        compiler_params=_params(2),
        name="qkv_proj",
    )(x3, g, w, bd, gv)


def _alibi_slope(head):
    return float(np.exp2(-8.0 * (head + 1) / N_HEADS))


def _attn_kernel(qkv_ref, o_ref, bias_ref, *, sub_len, dilation, group):
    n_blocks = sub_len // Q_BLOCK
    assert n_blocks & (n_blocks - 1) == 0
    offsets = (0, HALF_WINDOW, 2 * HALF_WINDOW)

    @pl.when(pl.program_id(0) == 0)
    def _build_bias():
        row = lax.broadcasted_iota(jnp.int32, (2 * Q_BLOCK, K_BLOCK), 0)
        key = lax.broadcasted_iota(jnp.int32, (2 * Q_BLOCK, K_BLOCK), 1)
        query = row & (Q_BLOCK - 1)
        for variant, off in enumerate(offsets):
            rel = jnp.abs(key - off - query)
            dist = (dilation * rel).astype(F32)
            for pair in range(2):
                head = group * HEADS_PER_GROUP + 2 * pair
                slope = jnp.where(row < Q_BLOCK, _alibi_slope(head), _alibi_slope(head + 1))
                bias_ref[variant, pair] = jnp.where(rel <= HALF_WINDOW, -(slope * dist), NEG_INF)

    lane = lax.broadcasted_iota(jnp.int32, (Q_BLOCK, LANES), 1)
    first_head = lane < HEAD_DIM
    lane_pair = lane >> 1
    ones = jnp.ones((K_BLOCK, LANES), BF16)

    def block(unit, carry):
        rho = lax.shift_right_logical(unit, n_blocks.bit_length() - 1)
        n = unit & (n_blocks - 1)
        q_start = pl.multiple_of(n * Q_BLOCK, Q_BLOCK)
        k_start = pl.multiple_of(jnp.clip(n * Q_BLOCK - HALF_WINDOW, 0, sub_len - K_BLOCK), HALF_WINDOW)
        variant = jnp.where(n == 0, 0, jnp.where(n == n_blocks - 1, 2, 1))
        lse_slab = jnp.zeros((Q_BLOCK, LANES), F32)
        for pair in range(2):
            base = pair * 3 * LANES
            q = qkv_ref[rho, pl.ds(q_start, Q_BLOCK), base:base + LANES]
            k = qkv_ref[rho, pl.ds(k_start, K_BLOCK), base + LANES:base + 2 * LANES]
            v = qkv_ref[rho, pl.ds(k_start, K_BLOCK), base + 2 * LANES:base + 3 * LANES]
            zero = jnp.zeros_like(q)
            qq = jnp.concatenate([jnp.where(first_head, q, zero), jnp.where(first_head, zero, q)], axis=0)
            s = lax.dot_general(qq, k, (((1,), (1,)), ((), ())), preferred_element_type=F32)
            s = s + bias_ref[variant, pair]
            m = jnp.max(s, axis=1, keepdims=True)
            e = jnp.exp(s - m).astype(BF16)
            pv = jnp.dot(e, jnp.concatenate([v, ones], axis=1), preferred_element_type=F32)
            l = pv[:, LANES:]
            o2 = pv[:, :LANES] * pl.reciprocal(l, approx=True)
            o_pair = jnp.where(first_head, o2[:Q_BLOCK], o2[Q_BLOCK:])
            o_ref[rho, pl.ds(q_start, Q_BLOCK), pair * LANES:(pair + 1) * LANES] = o_pair.astype(BF16)
            lse = m + jnp.log(l)
            for h2 in range(2):
                lse_slab = jnp.where(lane_pair == 2 * pair + h2, lse[h2 * Q_BLOCK:(h2 + 1) * Q_BLOCK], lse_slab)
        head_part = lse_slab.astype(BF16).astype(F32)
        lse_slab = jnp.where((lane & 1) == 1, lse_slab - head_part, lse_slab)
        o_ref[rho, pl.ds(q_start, Q_BLOCK), GROUP_WIDTH:] = lse_slab.astype(BF16)
        return carry

    lax.fori_loop(0, dilation * n_blocks, block, 0, unroll=ATTN_UNROLL)


def _attention_group(qkv_g, group):
    batch = qkv_g.shape[0]
    dilation = DILATIONS[group]
    sub_len = SEQ // dilation
    kern = functools.partial(_attn_kernel, sub_len=sub_len, dilation=dilation, group=group)
    return pl.pallas_call(
        kern,
        grid=(batch,),
        in_specs=[pl.BlockSpec((None, dilation, sub_len, QKV_GROUP_WIDTH), lambda b: (b, 0, 0, 0))],
        out_specs=pl.BlockSpec((None, dilation, sub_len, O_GROUP_WIDTH), lambda b: (b, 0, 0, 0)),
        out_shape=jax.ShapeDtypeStruct((batch, dilation, sub_len, O_GROUP_WIDTH), BF16),
        scratch_shapes=[pltpu.VMEM((3, 2, 2 * Q_BLOCK, K_BLOCK), F32)],
        compiler_params=_params(1),
        name=f"dilated_attn_g{group}",
    )(qkv_g)


def _attn_out_kernel(x_ref, o0_ref, o1_ref, o2_ref, w_ref, out_ref, tok_ref):
    tm = x_ref.shape[0]
    n_slabs = O_GROUP_WIDTH // LANES
    slabs = []
    for group, (o_ref, dilation) in enumerate(zip((o0_ref, o1_ref, o2_ref), DILATIONS)):
        for s in range(n_slabs):
            cols = slice(s * LANES, (s + 1) * LANES)
            if dilation == 1:
                slabs.append(o_ref[0, :, cols].astype(F32))
            else:
                slot = (group - 1) * n_slabs + s
                for rho in range(dilation):
                    tok_ref[slot, pl.ds(rho, tm // dilation, stride=dilation), :] = o_ref[rho, :, cols].astype(F32)
                slabs.append(tok_ref[slot])
    lses = [slabs[g * n_slabs + n_slabs - 1] for g in range(N_GROUPS)]
    lses = [t + pltpu.roll(t, LANES - 1, 1) for t in lses]
    top = jnp.maximum(jnp.maximum(lses[0], lses[1]), lses[2])
    es = [jnp.exp(t - top) for t in lses]
    inv = 1.0 / (es[0] + es[1] + es[2])
    lane = lax.broadcasted_iota(jnp.int32, (tm, GROUP_WIDTH), 1)
    acc = x_ref[...]
    for g in range(N_GROUPS):
        alpha = es[g] * inv
        wide = jnp.where(lane < HEAD_DIM, alpha[:, 0:1],
                         jnp.where(lane < 2 * HEAD_DIM, alpha[:, 2:3],
                                   jnp.where(lane < 3 * HEAD_DIM, alpha[:, 4:5], alpha[:, 6:7])))
        o = jnp.concatenate(slabs[g * n_slabs:g * n_slabs + n_slabs - 1], axis=1)
        acc = acc + jnp.dot((o * wide).astype(BF16), w_ref[g * GROUP_WIDTH:(g + 1) * GROUP_WIDTH, :],
                            preferred_element_type=F32)
    out_ref[...] = acc


def _attn_out(x3, os, w_o):
    batch = x3.shape[0]
    n_strided_slabs = (N_GROUPS - 1) * (O_GROUP_WIDTH // LANES)
    return pl.pallas_call(
        _attn_out_kernel,
        grid=(batch, SEQ // TM),
        in_specs=[pl.BlockSpec((None, TM, D_MODEL), lambda b, i: (b, i, 0))]
                 + [_residue_major_spec(r, TM, O_GROUP_WIDTH) for r in DILATIONS]
                 + [_resident((ATTN_WIDTH, D_MODEL))],
        out_specs=pl.BlockSpec((None, TM, D_MODEL), lambda b, i: (b, i, 0)),
        out_shape=jax.ShapeDtypeStruct(x3.shape, F32),
        scratch_shapes=[pltpu.VMEM((n_strided_slabs, TM, LANES), F32)],
        compiler_params=_params(2),
        name="attn_out_proj",
    )(x3, *os, w_o)


def _gmlp_kernel(x_ref, g_ref, win_ref, gv_ref, ws_ref, bs_ref, wout_ref, out_ref, gated_ref):
    x = x_ref[...]
    h = _rms(x, g_ref[...]).astype(BF16)
    z = jax.nn.gelu(jnp.dot(h, win_ref[...], preferred_element_type=F32))
    u = z[:, :D_MODEL]
    v = _rms(z[:, D_MODEL:], gv_ref[...]).astype(BF16)
    for n in range(x.shape[0] // CHUNK):
        rows = slice(n * CHUNK, (n + 1) * CHUNK)
        for sg in range(N_SPATIAL_GROUPS):
            cols = slice(sg * CHUNK, (sg + 1) * CHUNK)
            s = jnp.dot(ws_ref[sg], v[rows, cols], preferred_element_type=F32) + bs_ref[sg]
            gated_ref[rows, cols] = (u[rows, cols] * s).astype(BF16)
    out_ref[...] = x + jnp.dot(gated_ref[...], wout_ref[...], preferred_element_type=F32)


def _gmlp(x2, g, w_in, g_v, w_s, b_s, w_out):
    n = x2.shape[0]
    return pl.pallas_call(
        _gmlp_kernel,
        grid=(n // TM,),
        in_specs=[pl.BlockSpec((TM, D_MODEL), lambda i: (i, 0)),
                  _resident((1, D_MODEL)),
                  _resident((D_MODEL, 2 * D_MODEL)),
                  _resident((1, D_MODEL)),
                  _resident((N_SPATIAL_GROUPS, CHUNK, CHUNK)),
                  _resident((N_SPATIAL_GROUPS, CHUNK, CHUNK)),
                  _resident((D_MODEL, D_MODEL))],
        out_specs=pl.BlockSpec((TM, D_MODEL), lambda i: (i, 0)),
        out_shape=jax.ShapeDtypeStruct((n, D_MODEL), F32),
        scratch_shapes=[pltpu.VMEM((TM, D_MODEL), BF16)],
        compiler_params=_params(1),
        name="chunk_gmlp",
    )(x2, g, w_in, g_v, w_s, b_s, w_out)


def _ffn_kernel(xm_ref, xp_ref, xn_ref, g_ref, wup_ref, cw_ref, cb_ref, wdn_ref, out_ref, h_ref, a0_ref, a1_ref):
    i = pl.program_id(1)
    tm = xm_ref.shape[0]
    g = g_ref[...]
    xm = xm_ref[...]
    xp = jnp.where(i > 0, xp_ref[...], 0.0)
    xn = jnp.where(i < pl.num_programs(1) - 1, xn_ref[...], 0.0)
    h_ref[...] = jnp.concatenate([_rms(xp, g), _rms(xm, g), _rms(xn, g)], axis=0).astype(BF16)
    out_ref[...] = xm

    def up(c, a_ref):
        a_ref[...] = jnp.dot(h_ref[...], wup_ref[c], preferred_element_type=F32)

    def down(c, a_ref):
        cw = cw_ref[c]
        a = (cw[0:1] * a_ref[HALO - 1:HALO - 1 + tm, :] + cw[1:2] * a_ref[HALO:HALO + tm, :]
             + cw[2:3] * a_ref[HALO + 1:HALO + 1 + tm, :] + cb_ref[c])
        gate = a[:, :FF_CHUNK]
        val = a[:, FF_CHUNK:]
        act = gate * pl.reciprocal(1.0 + jnp.exp(-gate), approx=True) * val
        out_ref[...] += jnp.dot(act.astype(BF16), wdn_ref[c], preferred_element_type=F32)

    up(0, a0_ref)

    def two_chunks(k, carry):
        c = 2 * k
        up(c + 1, a1_ref)
        down(c, a0_ref)
        up(c + 2, a0_ref)
        down(c + 1, a1_ref)
        return carry

    assert N_FF_CHUNKS % 2 == 1
    lax.fori_loop(0, N_FF_CHUNKS // 2, two_chunks, 0)
    down(N_FF_CHUNKS - 1, a0_ref)


def _ffn(x3, g, w_up, conv_w, conv_b, w_down):
    batch = x3.shape[0]
    tiles = SEQ // TM
    halo_blocks = TM // HALO
    last_halo = SEQ // HALO - 1
    return pl.pallas_call(
        _ffn_kernel,
        grid=(batch, tiles),
        in_specs=[pl.BlockSpec((None, TM, D_MODEL), lambda b, i: (b, i, 0)),
                  pl.BlockSpec((None, HALO, D_MODEL),
                               lambda b, i: (b, jnp.maximum(i * halo_blocks - 1, 0), 0)),
                  pl.BlockSpec((None, HALO, D_MODEL),
                               lambda b, i: (b, jnp.minimum((i + 1) * halo_blocks, last_halo), 0)),
                  _resident((1, D_MODEL)),
                  _resident((N_FF_CHUNKS, D_MODEL, 2 * FF_CHUNK)),
                  _resident((N_FF_CHUNKS, 3, 2 * FF_CHUNK)),
                  _resident((N_FF_CHUNKS, 1, 2 * FF_CHUNK)),
                  _resident((N_FF_CHUNKS, FF_CHUNK, D_MODEL))],
        out_specs=pl.BlockSpec((None, TM, D_MODEL), lambda b, i: (b, i, 0)),
        out_shape=jax.ShapeDtypeStruct(x3.shape, F32),
        scratch_shapes=[pltpu.VMEM((TM + 2 * HALO, D_MODEL), BF16),
                        pltpu.VMEM((TM + 2 * HALO, 2 * FF_CHUNK), F32),
                        pltpu.VMEM((TM + 2 * HALO, 2 * FF_CHUNK), F32)],
        compiler_params=_params(2),
        name="conv_gated_ffn",
    )(x3, x3, x3, g, w_up, conv_w, conv_b, w_down)


def _qkv_column_order():
    cols = []
    for group in range(N_GROUPS):
        for pair in range(2):
            for part in range(3):
                for h2 in range(2):
                    head = group * HEADS_PER_GROUP + 2 * pair + h2
                    start = part * ATTN_WIDTH + head * HEAD_DIM
                    cols.extend(range(start, start + HEAD_DIM))
    return np.asarray(cols, dtype=np.int32)


def _chunk_gate_val(t):
    lead = t.shape[:-1]
    gv = t.reshape(lead + (2, N_FF_CHUNKS, FF_CHUNK))
    gv = jnp.moveaxis(gv, -2, 0)
    return gv.reshape((N_FF_CHUNKS,) + lead + (2 * FF_CHUNK,))


def _prep_ffn(layer, ffn_norm_g, ffn_w_up, ffn_conv_w, ffn_conv_b, ffn_w_down):
    return (ffn_norm_g[layer][None, :],
            _chunk_gate_val(ffn_w_up[layer]).astype(BF16),
            _chunk_gate_val(ffn_conv_w[layer]),
            _chunk_gate_val(ffn_conv_b[layer][None, :]),
            ffn_w_down[layer].reshape(N_FF_CHUNKS, FF_CHUNK, D_MODEL).astype(BF16))


def _trunk(x, attn_w, gmlp_w, ffn_w0, ffn_w1):
    batch = x.shape[0]
    n = batch * SEQ

    norm_g, w_qkv, block_avg, qk_gain, w_o = attn_w
    qkvs = _qkv_proj(x, norm_g, w_qkv, block_avg, qk_gain)
    os = [_attention_group(qkv_g, group) for group, qkv_g in enumerate(qkvs)]
    x = _attn_out(x, os, w_o)
    x = _ffn(x, *ffn_w0)

    x2 = _gmlp(x.reshape(n, D_MODEL), *gmlp_w)
    return _ffn(x2.reshape(batch, SEQ, D_MODEL), *ffn_w1)


def kernel(x_prompt, x_sample, attn_norm_g, attn_w_qkv, attn_q_norm_g, attn_k_norm_g, attn_w_o, gmlp_norm_g, gmlp_w_in, gmlp_v_norm_g, gmlp_w_spatial, gmlp_b_spatial, gmlp_w_out, ffn_norm_g, ffn_w_up, ffn_conv_w, ffn_conv_b, ffn_w_down):
    head_of = np.arange(2 * LANES) // HEAD_DIM
    block_avg = jnp.asarray((head_of[:, None] == head_of[None, :]) / HEAD_DIM, dtype=BF16)
    qk_gain = jnp.concatenate([jnp.tile(attn_q_norm_g[0] * HEAD_DIM ** -0.5, 2),
                               jnp.tile(attn_k_norm_g[0], 2)])[None, :]
    attn_w = (attn_norm_g[0][None, :],
              attn_w_qkv[0][:, _qkv_column_order()].astype(BF16),
              block_avg, qk_gain,
              attn_w_o[0].astype(BF16))
    gmlp_w = (gmlp_norm_g[0][None, :],
              gmlp_w_in[0].astype(BF16),
              gmlp_v_norm_g[0][None, :],
              gmlp_w_spatial[0].astype(BF16),
              jnp.broadcast_to(gmlp_b_spatial[0][:, :, None], (N_SPATIAL_GROUPS, CHUNK, CHUNK)),
              gmlp_w_out[0].astype(BF16))
    ffn_w0 = _prep_ffn(0, ffn_norm_g, ffn_w_up, ffn_conv_w, ffn_conv_b, ffn_w_down)
    ffn_w1 = _prep_ffn(1, ffn_norm_g, ffn_w_up, ffn_conv_w, ffn_conv_b, ffn_w_down)
    return (_trunk(x_prompt, attn_w, gmlp_w, ffn_w0, ffn_w1),
            _trunk(x_sample, attn_w, gmlp_w, ffn_w0, ffn_w1))
```

```python
import functools

import numpy as np
import jax
import jax.numpy as jnp
from jax import lax
from jax.experimental import pallas as pl
from jax.experimental.pallas import tpu as pltpu

F32 = jnp.float32
BF16 = jnp.bfloat16

D_MODEL = 1024
SEQ = 4096
HEAD_DIM = 64
DILATIONS = (1, 4, 16)
N_GROUPS = len(DILATIONS)
HALF_WINDOW = 64
HEADS_PER_GROUP = 4
N_HEADS = N_GROUPS * HEADS_PER_GROUP
ATTN_WIDTH = N_HEADS * HEAD_DIM
GROUP_WIDTH = HEADS_PER_GROUP * HEAD_DIM
LANES = 128
PAIR_WIDTH = 2 * HEAD_DIM
assert PAIR_WIDTH == LANES
QKV_GROUP_WIDTH = 3 * GROUP_WIDTH
O_GROUP_WIDTH = GROUP_WIDTH + LANES
CHUNK = 128
N_SPATIAL_GROUPS = 8
D_FF = 2816
EPS = 1e-6
NEG_INF = -1e30

Q_BLOCK = 128
K_BLOCK = Q_BLOCK + 2 * HALF_WINDOW
ATTN_UNROLL = 4
FF_CHUNK = 256
N_FF_CHUNKS = D_FF // FF_CHUNK
SUBLANES = 8
HALO = SUBLANES
TM = 512
FFN_STREAMS = 2
VMEM_LIMIT = 56 * 1024 * 1024


def _rms(x, g):
    ms = jnp.mean(x * x, axis=-1, keepdims=True)
    return x * lax.rsqrt(ms + EPS) * g


def _params(n_axes, flags=None):
    return pltpu.CompilerParams(dimension_semantics=("arbitrary",) * n_axes,
                                vmem_limit_bytes=VMEM_LIMIT, flags=flags)


def _resident(shape):
    nd = len(shape)
    return pl.BlockSpec(shape, lambda *_: (0,) * nd, pipeline_mode=pl.Buffered(1))


def _residue_major_spec(dilation, rows, width):
    return pl.BlockSpec((None, dilation, rows // dilation, width), lambda b, i: (b, 0, i, 0))


def _qkv_kernel(x_ref, g_ref, w_ref, bd_ref, gv_ref, o0_ref, o1_ref, o2_ref, y_ref):
    tm = x_ref.shape[0]
    h = _rms(x_ref[...], g_ref[...]).astype(BF16)
    out_refs = (o0_ref, o1_ref, o2_ref)
    for group, dilation in enumerate(DILATIONS):
        for pair in range(2):
            col0 = group * QKV_GROUP_WIDTH + pair * 3 * LANES
            y = jnp.dot(h, w_ref[:, col0:col0 + 3 * LANES], preferred_element_type=F32)
            qk = y[:, :2 * LANES]
            ms = jnp.dot((qk * qk).astype(BF16), bd_ref[...], preferred_element_type=F32)
            qkn = qk * lax.rsqrt(ms + EPS) * gv_ref[...]
            slabs = (qkn[:, :LANES], qkn[:, LANES:], y[:, 2 * LANES:])
            out_ref = out_refs[group]
            for s, slab in enumerate(slabs):
                cols = slice((pair * 3 + s) * LANES, (pair * 3 + s + 1) * LANES)
                if dilation == 1:
                    out_ref[0, :, cols] = slab.astype(BF16)
                else:
                    slot = ((group - 1) * 2 + pair) * 3 + s
                    y_ref[slot] = slab
                    for rho in range(dilation):
                        rows = y_ref[slot, pl.ds(rho, tm // dilation, stride=dilation), :]
                        out_ref[rho, :, cols] = rows.astype(BF16)


def _qkv_proj(x3, g, w, bd, gv):
    batch = x3.shape[0]
    n_strided_slabs = (N_GROUPS - 1) * 2 * 3
    return pl.pallas_call(
        _qkv_kernel,
        grid=(batch, SEQ // TM),
        in_specs=[pl.BlockSpec((None, TM, D_MODEL), lambda b, i: (b, i, 0)),
                  _resident((1, D_MODEL)),
                  _resident((D_MODEL, 3 * ATTN_WIDTH)),
                  _resident((2 * LANES, 2 * LANES)),
                  _resident((1, 2 * LANES))],
        out_specs=[_residue_major_spec(r, TM, QKV_GROUP_WIDTH) for r in DILATIONS],
        out_shape=[jax.ShapeDtypeStruct((batch, r, SEQ // r, QKV_GROUP_WIDTH), BF16) for r in DILATIONS],
        scratch_shapes=[pltpu.VMEM((n_strided_slabs, TM, LANES), F32)],
        compiler_params=_params(2),
        name="qkv_proj",
    )(x3, g, w, bd, gv)


def _alibi_slope(head):
    return float(np.exp2(-8.0 * (head + 1) / N_HEADS))


def _attn_kernel(qkv_ref, o_ref, bias_ref, *, sub_len, dilation, group):
    n_blocks = sub_len // Q_BLOCK
    assert n_blocks & (n_blocks - 1) == 0
    offsets = (0, HALF_WINDOW, 2 * HALF_WINDOW)

    @pl.when(pl.program_id(0) == 0)
    def _build_bias():
        row = lax.broadcasted_iota(jnp.int32, (2 * Q_BLOCK, K_BLOCK), 0)
        key = lax.broadcasted_iota(jnp.int32, (2 * Q_BLOCK, K_BLOCK), 1)
        query = row & (Q_BLOCK - 1)
        for variant, off in enumerate(offsets):
            rel = jnp.abs(key - off - query)
            dist = (dilation * rel).astype(F32)
            for pair in range(2):
                head = group * HEADS_PER_GROUP + 2 * pair
                slope = jnp.where(row < Q_BLOCK, _alibi_slope(head), _alibi_slope(head + 1))
                bias_ref[variant, pair] = jnp.where(rel <= HALF_WINDOW, -(slope * dist), NEG_INF)

    lane = lax.broadcasted_iota(jnp.int32, (Q_BLOCK, LANES), 1)
    first_head = lane < HEAD_DIM
    lane_pair = lane >> 1
    ones = jnp.ones((K_BLOCK, LANES), BF16)

    def block(unit, carry):
        rho = lax.shift_right_logical(unit, n_blocks.bit_length() - 1)
        n = unit & (n_blocks - 1)
        q_start = pl.multiple_of(n * Q_BLOCK, Q_BLOCK)
        k_start = pl.multiple_of(jnp.clip(n * Q_BLOCK - HALF_WINDOW, 0, sub_len - K_BLOCK), HALF_WINDOW)
        variant = jnp.where(n == 0, 0, jnp.where(n == n_blocks - 1, 2, 1))
        lse_slab = jnp.zeros((Q_BLOCK, LANES), F32)
        for pair in range(2):
            base = pair * 3 * LANES
            q = qkv_ref[rho, pl.ds(q_start, Q_BLOCK), base:base + LANES]
            k = qkv_ref[rho, pl.ds(k_start, K_BLOCK), base + LANES:base + 2 * LANES]
            v = qkv_ref[rho, pl.ds(k_start, K_BLOCK), base + 2 * LANES:base + 3 * LANES]
            zero = jnp.zeros_like(q)
            qq = jnp.concatenate([jnp.where(first_head, q, zero), jnp.where(first_head, zero, q)], axis=0)
            s = lax.dot_general(qq, k, (((1,), (1,)), ((), ())), preferred_element_type=F32)
            s = s + bias_ref[variant, pair]
            m = jnp.max(s, axis=1, keepdims=True)
            e = jnp.exp(s - m).astype(BF16)
            pv = jnp.dot(e, jnp.concatenate([v, ones], axis=1), preferred_element_type=F32)
            l = pv[:, LANES:]
            o2 = pv[:, :LANES] * pl.reciprocal(l, approx=True)
            o_pair = jnp.where(first_head, o2[:Q_BLOCK], o2[Q_BLOCK:])
            o_ref[rho, pl.ds(q_start, Q_BLOCK), pair * LANES:(pair + 1) * LANES] = o_pair.astype(BF16)
            lse = m + jnp.log(l)
            for h2 in range(2):
                lse_slab = jnp.where(lane_pair == 2 * pair + h2, lse[h2 * Q_BLOCK:(h2 + 1) * Q_BLOCK], lse_slab)
        head_part = lse_slab.astype(BF16).astype(F32)
        lse_slab = jnp.where((lane & 1) == 1, lse_slab - head_part, lse_slab)
        o_ref[rho, pl.ds(q_start, Q_BLOCK), GROUP_WIDTH:] = lse_slab.astype(BF16)
        return carry

    lax.fori_loop(0, dilation * n_blocks, block, 0, unroll=ATTN_UNROLL)


def _attention_group(qkv_g, group):
    batch = qkv_g.shape[0]
    dilation = DILATIONS[group]
    sub_len = SEQ // dilation
    kern = functools.partial(_attn_kernel, sub_len=sub_len, dilation=dilation, group=group)
    return pl.pallas_call(
        kern,
        grid=(batch,),
        in_specs=[pl.BlockSpec((None, dilation, sub_len, QKV_GROUP_WIDTH), lambda b: (b, 0, 0, 0))],
        out_specs=pl.BlockSpec((None, dilation, sub_len, O_GROUP_WIDTH), lambda b: (b, 0, 0, 0)),
        out_shape=jax.ShapeDtypeStruct((batch, dilation, sub_len, O_GROUP_WIDTH), BF16),
        scratch_shapes=[pltpu.VMEM((3, 2, 2 * Q_BLOCK, K_BLOCK), F32)],
        compiler_params=_params(1),
        name=f"dilated_attn_g{group}",
    )(qkv_g)


def _attn_out_kernel(x_ref, o0_ref, o1_ref, o2_ref, w_ref, out_ref, tok_ref):
    tm = x_ref.shape[0]
    n_slabs = O_GROUP_WIDTH // LANES
    slabs = []
    for group, (o_ref, dilation) in enumerate(zip((o0_ref, o1_ref, o2_ref), DILATIONS)):
        for s in range(n_slabs):
            cols = slice(s * LANES, (s + 1) * LANES)
            if dilation == 1:
                slabs.append(o_ref[0, :, cols].astype(F32))
            else:
                slot = (group - 1) * n_slabs + s
                for rho in range(dilation):
                    tok_ref[slot, pl.ds(rho, tm // dilation, stride=dilation), :] = o_ref[rho, :, cols].astype(F32)
                slabs.append(tok_ref[slot])
    lses = [slabs[g * n_slabs + n_slabs - 1] for g in range(N_GROUPS)]
    lses = [t + pltpu.roll(t, LANES - 1, 1) for t in lses]
    top = jnp.maximum(jnp.maximum(lses[0], lses[1]), lses[2])
    es = [jnp.exp(t - top) for t in lses]
    inv = 1.0 / (es[0] + es[1] + es[2])
    lane = lax.broadcasted_iota(jnp.int32, (tm, GROUP_WIDTH), 1)
    acc = x_ref[...]
    for g in range(N_GROUPS):
        alpha = es[g] * inv
        wide = jnp.where(lane < HEAD_DIM, alpha[:, 0:1],
                         jnp.where(lane < 2 * HEAD_DIM, alpha[:, 2:3],
                                   jnp.where(lane < 3 * HEAD_DIM, alpha[:, 4:5], alpha[:, 6:7])))
        o = jnp.concatenate(slabs[g * n_slabs:g * n_slabs + n_slabs - 1], axis=1)
        acc = acc + jnp.dot((o * wide).astype(BF16), w_ref[g * GROUP_WIDTH:(g + 1) * GROUP_WIDTH, :],
                            preferred_element_type=F32)
    out_ref[...] = acc


def _attn_out(x3, os, w_o):
    batch = x3.shape[0]
    n_strided_slabs = (N_GROUPS - 1) * (O_GROUP_WIDTH // LANES)
    return pl.pallas_call(
        _attn_out_kernel,
        grid=(batch, SEQ // TM),
        in_specs=[pl.BlockSpec((None, TM, D_MODEL), lambda b, i: (b, i, 0))]
                 + [_residue_major_spec(r, TM, O_GROUP_WIDTH) for r in DILATIONS]
                 + [_resident((ATTN_WIDTH, D_MODEL))],
        out_specs=pl.BlockSpec((None, TM, D_MODEL), lambda b, i: (b, i, 0)),
        out_shape=jax.ShapeDtypeStruct(x3.shape, F32),
        scratch_shapes=[pltpu.VMEM((n_strided_slabs, TM, LANES), F32)],
        compiler_params=_params(2),
        name="attn_out_proj",
    )(x3, *os, w_o)


def _gmlp_kernel(x_ref, g_ref, win_ref, gv_ref, ws_ref, bs_ref, wout_ref, out_ref, gated_ref):
    x = x_ref[...]
    h = _rms(x, g_ref[...]).astype(BF16)
    z = jax.nn.gelu(jnp.dot(h, win_ref[...], preferred_element_type=F32))
    u = z[:, :D_MODEL]
    v = _rms(z[:, D_MODEL:], gv_ref[...]).astype(BF16)
    for n in range(x.shape[0] // CHUNK):
        rows = slice(n * CHUNK, (n + 1) * CHUNK)
        for sg in range(N_SPATIAL_GROUPS):
            cols = slice(sg * CHUNK, (sg + 1) * CHUNK)
            s = jnp.dot(ws_ref[sg], v[rows, cols], preferred_element_type=F32) + bs_ref[sg]
            gated_ref[rows, cols] = (u[rows, cols] * s).astype(BF16)
    out_ref[...] = x + jnp.dot(gated_ref[...], wout_ref[...], preferred_element_type=F32)


def _gmlp(x2, g, w_in, g_v, w_s, b_s, w_out):
    n = x2.shape[0]
    return pl.pallas_call(
        _gmlp_kernel,
        grid=(n // TM,),
        in_specs=[pl.BlockSpec((TM, D_MODEL), lambda i: (i, 0)),
                  _resident((1, D_MODEL)),
                  _resident((D_MODEL, 2 * D_MODEL)),
                  _resident((1, D_MODEL)),
                  _resident((N_SPATIAL_GROUPS, CHUNK, CHUNK)),
                  _resident((N_SPATIAL_GROUPS, CHUNK, CHUNK)),
                  _resident((D_MODEL, D_MODEL))],
        out_specs=pl.BlockSpec((TM, D_MODEL), lambda i: (i, 0)),
        out_shape=jax.ShapeDtypeStruct((n, D_MODEL), F32),
        scratch_shapes=[pltpu.VMEM((TM, D_MODEL), BF16)],
        compiler_params=_params(1),
        name="chunk_gmlp",
    )(x2, g, w_in, g_v, w_s, b_s, w_out)


def _ffn_kernel(xm_ref, xp_ref, xn_ref, g_ref, wup_ref, cw_ref, cb_ref, wdn_ref, out_ref,
                acc_ref, h_ref, a_ref, c0_ref, c1_ref):
    i = pl.program_id(1)
    tm = xm_ref.shape[1]
    seg = tm // SUBLANES
    n_slabs = D_MODEL // LANES
    g = g_ref[...]
    streams = range(FFN_STREAMS)
    for u in streams:
        for s in range(SUBLANES):
            for slab in range(n_slabs):
                acc_ref[u, slab, pl.ds(s, seg, stride=SUBLANES), :] = (
                    xm_ref[u, s * seg:(s + 1) * seg, slab * LANES:(slab + 1) * LANES])
        xm = jnp.concatenate([acc_ref[u, slab] for slab in range(n_slabs)], axis=1)
        xp = jnp.where(i > 0, xp_ref[u], 0.0)
        xn = jnp.where(i < pl.num_programs(1) - 1, xn_ref[u], 0.0)
        h_ref[u] = jnp.concatenate([_rms(xp, g), _rms(xm, g), _rms(xn, g)], axis=0).astype(BF16)
    sublane = lax.broadcasted_iota(jnp.int32, (SUBLANES, 2 * FF_CHUNK), 0)

    def up(c, c_ref):
        cw = cw_ref[c]
        for u in streams:
            a = a_ref.at[u]
            a[...] = jnp.dot(h_ref[u], wup_ref[c], preferred_element_type=F32)
            a[0:HALO] = pltpu.roll(
                jnp.where(sublane == SUBLANES - 1, a[0:HALO], a[tm:tm + HALO]), 1, 0)
            a[tm + HALO:tm + 2 * HALO] = pltpu.roll(
                jnp.where(sublane == 0, a[tm + HALO:tm + 2 * HALO], a[HALO:2 * HALO]), SUBLANES - 1, 0)
            c_ref[u] = (cw[0:1] * a[0:tm, :] + cw[1:2] * a[HALO:HALO + tm, :]
                        + cw[2:3] * a[2 * HALO:2 * HALO + tm, :] + cb_ref[c])

    def down(c, c_ref):
        for u in streams:
            gate = c_ref[u, :, :FF_CHUNK]
            val = c_ref[u, :, FF_CHUNK:]
            act = gate * pl.reciprocal(1.0 + jnp.exp(-gate), approx=True) * val
            y = jnp.dot(act.astype(BF16), wdn_ref[c], preferred_element_type=F32)
            for slab in range(n_slabs):
                acc_ref[u, slab] += y[:, slab * LANES:(slab + 1) * LANES]

    up(0, c0_ref)

    def two_chunks(k, carry):
        c = 2 * k
        up(c + 1, c1_ref)
        down(c, c0_ref)
        up(c + 2, c0_ref)
        down(c + 1, c1_ref)
        return carry

    assert N_FF_CHUNKS % 2 == 1
    lax.fori_loop(0, N_FF_CHUNKS // 2, two_chunks, 0)
    down(N_FF_CHUNKS - 1, c0_ref)

    for u in streams:
        for s in range(SUBLANES):
            for slab in range(n_slabs):
                out_ref[u, s * seg:(s + 1) * seg, slab * LANES:(slab + 1) * LANES] = (
                    acc_ref[u, slab, pl.ds(s, seg, stride=SUBLANES), :])


def _ffn(x3, g, w_up, conv_w, conv_b, w_down):
    batch = x3.shape[0]
    tiles = SEQ // TM
    halo_blocks = TM // HALO
    last_halo = SEQ // HALO - 1
    return pl.pallas_call(
        _ffn_kernel,
        grid=(batch // FFN_STREAMS, tiles),
        in_specs=[pl.BlockSpec((FFN_STREAMS, TM, D_MODEL), lambda b, i: (b, i, 0)),
                  pl.BlockSpec((FFN_STREAMS, HALO, D_MODEL),
                               lambda b, i: (b, jnp.maximum(i * halo_blocks - 1, 0), 0)),
                  pl.BlockSpec((FFN_STREAMS, HALO, D_MODEL),
                               lambda b, i: (b, jnp.minimum((i + 1) * halo_blocks, last_halo), 0)),
                  _resident((1, D_MODEL)),
                  _resident((N_FF_CHUNKS, D_MODEL, 2 * FF_CHUNK)),
                  _resident((N_FF_CHUNKS, 3, 2 * FF_CHUNK)),
                  _resident((N_FF_CHUNKS, 1, 2 * FF_CHUNK)),
                  _resident((N_FF_CHUNKS, FF_CHUNK, D_MODEL))],
        out_specs=pl.BlockSpec((FFN_STREAMS, TM, D_MODEL), lambda b, i: (b, i, 0)),
        out_shape=jax.ShapeDtypeStruct(x3.shape, F32),
        scratch_shapes=[pltpu.VMEM((FFN_STREAMS, D_MODEL // LANES, TM, LANES), F32),
                        pltpu.VMEM((FFN_STREAMS, TM + 2 * HALO, D_MODEL), BF16),
                        pltpu.VMEM((FFN_STREAMS, TM + 2 * HALO, 2 * FF_CHUNK), F32),
                        pltpu.VMEM((FFN_STREAMS, TM, 2 * FF_CHUNK), F32),
                        pltpu.VMEM((FFN_STREAMS, TM, 2 * FF_CHUNK), F32)],
        compiler_params=_params(2),
        name="conv_gated_ffn",
    )(x3, x3, x3, g, w_up, conv_w, conv_b, w_down)


def _qkv_column_order():
    cols = []
    for group in range(N_GROUPS):
        for pair in range(2):
            for part in range(3):
                for h2 in range(2):
                    head = group * HEADS_PER_GROUP + 2 * pair + h2
                    start = part * ATTN_WIDTH + head * HEAD_DIM
                    cols.extend(range(start, start + HEAD_DIM))
    return np.asarray(cols, dtype=np.int32)


def _chunk_gate_val(t):
    lead = t.shape[:-1]
    gv = t.reshape(lead + (2, N_FF_CHUNKS, FF_CHUNK))
    gv = jnp.moveaxis(gv, -2, 0)
    return gv.reshape((N_FF_CHUNKS,) + lead + (2 * FF_CHUNK,))


def _prep_ffn(layer, ffn_norm_g, ffn_w_up, ffn_conv_w, ffn_conv_b, ffn_w_down):
    return (ffn_norm_g[layer][None, :],
            _chunk_gate_val(ffn_w_up[layer]).astype(BF16),
            _chunk_gate_val(ffn_conv_w[layer]),
            _chunk_gate_val(ffn_conv_b[layer][None, :]),
            ffn_w_down[layer].reshape(N_FF_CHUNKS, FF_CHUNK, D_MODEL).astype(BF16))


def _trunk(x, attn_w, gmlp_w, ffn_w0, ffn_w1):
    batch = x.shape[0]
    n = batch * SEQ

    norm_g, w_qkv, block_avg, qk_gain, w_o = attn_w
    qkvs = _qkv_proj(x, norm_g, w_qkv, block_avg, qk_gain)
    os = [_attention_group(qkv_g, group) for group, qkv_g in enumerate(qkvs)]
    x = _attn_out(x, os, w_o)
    x = _ffn(x, *ffn_w0)

    x2 = _gmlp(x.reshape(n, D_MODEL), *gmlp_w)
    return _ffn(x2.reshape(batch, SEQ, D_MODEL), *ffn_w1)


def kernel(x_prompt, x_sample, attn_norm_g, attn_w_qkv, attn_q_norm_g, attn_k_norm_g, attn_w_o, gmlp_norm_g, gmlp_w_in, gmlp_v_norm_g, gmlp_w_spatial, gmlp_b_spatial, gmlp_w_out, ffn_norm_g, ffn_w_up, ffn_conv_w, ffn_conv_b, ffn_w_down):
    head_of = np.arange(2 * LANES) // HEAD_DIM
    block_avg = jnp.asarray((head_of[:, None] == head_of[None, :]) / HEAD_DIM, dtype=BF16)
    qk_gain = jnp.concatenate([jnp.tile(attn_q_norm_g[0] * HEAD_DIM ** -0.5, 2),
                               jnp.tile(attn_k_norm_g[0], 2)])[None, :]
    attn_w = (attn_norm_g[0][None, :],
              attn_w_qkv[0][:, _qkv_column_order()].astype(BF16),
              block_avg, qk_gain,
              attn_w_o[0].astype(BF16))
    gmlp_w = (gmlp_norm_g[0][None, :],
              gmlp_w_in[0].astype(BF16),
              gmlp_v_norm_g[0][None, :],
              gmlp_w_spatial[0].astype(BF16),
              jnp.broadcast_to(gmlp_b_spatial[0][:, :, None], (N_SPATIAL_GROUPS, CHUNK, CHUNK)),
              gmlp_w_out[0].astype(BF16))
    ffn_w0 = _prep_ffn(0, ffn_norm_g, ffn_w_up, ffn_conv_w, ffn_conv_b, ffn_w_down)
    ffn_w1 = _prep_ffn(1, ffn_norm_g, ffn_w_up, ffn_conv_w, ffn_conv_b, ffn_w_down)
    return (_trunk(x_prompt, attn_w, gmlp_w, ffn_w0, ffn_w1),
            _trunk(x_sample, attn_w, gmlp_w, ffn_w0, ffn_w1))
```

```python
import functools

import numpy as np
import jax
import jax.numpy as jnp
from jax import lax
from jax.experimental import pallas as pl
from jax.experimental.pallas import tpu as pltpu

F32 = jnp.float32
BF16 = jnp.bfloat16

D_MODEL = 1024
SEQ = 4096
HEAD_DIM = 64
DILATIONS = (1, 4, 16)
N_GROUPS = len(DILATIONS)
HALF_WINDOW = 64
HEADS_PER_GROUP = 4
N_HEADS = N_GROUPS * HEADS_PER_GROUP
ATTN_WIDTH = N_HEADS * HEAD_DIM
GROUP_WIDTH = HEADS_PER_GROUP * HEAD_DIM
LANES = 128
PAIR_WIDTH = 2 * HEAD_DIM
assert PAIR_WIDTH == LANES
QKV_GROUP_WIDTH = 3 * GROUP_WIDTH
O_GROUP_WIDTH = GROUP_WIDTH + LANES
CHUNK = 128
N_SPATIAL_GROUPS = 8
D_FF = 2816
EPS = 1e-6
NEG_INF = -1e30

Q_BLOCK = 128
K_BLOCK = Q_BLOCK + 2 * HALF_WINDOW
ATTN_UNROLL = 8
FF_CHUNK = 256
N_FF_CHUNKS = D_FF // FF_CHUNK
SUBLANES = 8
HALO = SUBLANES
TM = 512
TM_ROW = 1024
FFN_STREAMS = 2
VMEM_LIMIT = 56 * 1024 * 1024


def _rms(x, g):
    ms = jnp.mean(x * x, axis=-1, keepdims=True)
    return x * lax.rsqrt(ms + EPS) * g


def _params(n_axes, flags=None):
    return pltpu.CompilerParams(dimension_semantics=("arbitrary",) * n_axes,
                                vmem_limit_bytes=VMEM_LIMIT, flags=flags)


def _resident(shape):
    nd = len(shape)
    return pl.BlockSpec(shape, lambda *_: (0,) * nd, pipeline_mode=pl.Buffered(1))


def _residue_major_spec(dilation, rows, width):
    return pl.BlockSpec((None, dilation, rows // dilation, width), lambda b, i: (b, 0, i, 0))


def _qkv_kernel(x_ref, g_ref, w_ref, bd_ref, gv_ref, o0_ref, o1_ref, o2_ref, y_ref):
    tm = x_ref.shape[0]
    h = _rms(x_ref[...], g_ref[...]).astype(BF16)
    out_refs = (o0_ref, o1_ref, o2_ref)
    for group, dilation in enumerate(DILATIONS):
        for pair in range(2):
            col0 = group * QKV_GROUP_WIDTH + pair * 3 * LANES
            y = jnp.dot(h, w_ref[:, col0:col0 + 3 * LANES], preferred_element_type=F32)
            qk = y[:, :2 * LANES]
            ms = jnp.dot((qk * qk).astype(BF16), bd_ref[...], preferred_element_type=F32)
            qkn = qk * lax.rsqrt(ms + EPS) * gv_ref[...]
            slabs = (qkn[:, :LANES], qkn[:, LANES:], y[:, 2 * LANES:])
            out_ref = out_refs[group]
            for s, slab in enumerate(slabs):
                cols = slice((pair * 3 + s) * LANES, (pair * 3 + s + 1) * LANES)
                if dilation == 1:
                    out_ref[0, :, cols] = slab.astype(BF16)
                else:
                    slot = ((group - 1) * 2 + pair) * 3 + s
                    y_ref[slot] = slab
                    for rho in range(dilation):
                        rows = y_ref[slot, pl.ds(rho, tm // dilation, stride=dilation), :]
                        out_ref[rho, :, cols] = rows.astype(BF16)


def _qkv_proj(x3, g, w, bd, gv):
    batch = x3.shape[0]
    n_strided_slabs = (N_GROUPS - 1) * 2 * 3
    return pl.pallas_call(
        _qkv_kernel,
        grid=(batch, SEQ // TM_ROW),
        in_specs=[pl.BlockSpec((None, TM_ROW, D_MODEL), lambda b, i: (b, i, 0)),
                  _resident((1, D_MODEL)),
                  _resident((D_MODEL, 3 * ATTN_WIDTH)),
                  _resident((2 * LANES, 2 * LANES)),
                  _resident((1, 2 * LANES))],
        out_specs=[_residue_major_spec(r, TM_ROW, QKV_GROUP_WIDTH) for r in DILATIONS],
        out_shape=[jax.ShapeDtypeStruct((batch, r, SEQ // r, QKV_GROUP_WIDTH), BF16) for r in DILATIONS],
        scratch_shapes=[pltpu.VMEM((n_strided_slabs, TM_ROW, LANES), F32)],
        compiler_params=_params(2),
        name="qkv_proj",
    )(x3, g, w, bd, gv)


def _alibi_slope(head):
    return float(np.exp2(-8.0 * (head + 1) / N_HEADS))


def _attn_kernel(qkv_ref, o_ref, bias_ref, *, sub_len, dilation, group):
    n_blocks = sub_len // Q_BLOCK
    assert n_blocks & (n_blocks - 1) == 0
    offsets = (0, HALF_WINDOW, 2 * HALF_WINDOW)

    @pl.when(pl.program_id(0) == 0)
    def _build_bias():
        row = lax.broadcasted_iota(jnp.int32, (2 * Q_BLOCK, K_BLOCK), 0)
        key = lax.broadcasted_iota(jnp.int32, (2 * Q_BLOCK, K_BLOCK), 1)
        query = row & (Q_BLOCK - 1)
        for variant, off in enumerate(offsets):
            rel = jnp.abs(key - off - query)
            dist = (dilation * rel).astype(F32)
            for pair in range(2):
                head = group * HEADS_PER_GROUP + 2 * pair
                slope = jnp.where(row < Q_BLOCK, _alibi_slope(head), _alibi_slope(head + 1))
                bias_ref[variant, pair] = jnp.where(rel <= HALF_WINDOW, -(slope * dist), NEG_INF)

    lane = lax.broadcasted_iota(jnp.int32, (Q_BLOCK, LANES), 1)
    first_head = lane < HEAD_DIM
    lane_pair = lane >> 1
    ones = jnp.ones((K_BLOCK, LANES), BF16)

    def block(unit, carry):
        rho = lax.shift_right_logical(unit, n_blocks.bit_length() - 1)
        n = unit & (n_blocks - 1)
        q_start = pl.multiple_of(n * Q_BLOCK, Q_BLOCK)
        k_start = pl.multiple_of(jnp.clip(n * Q_BLOCK - HALF_WINDOW, 0, sub_len - K_BLOCK), HALF_WINDOW)
        variant = jnp.where(n == 0, 0, jnp.where(n == n_blocks - 1, 2, 1))
        lse_slab = jnp.zeros((Q_BLOCK, LANES), F32)
        for pair in range(2):
            base = pair * 3 * LANES
            q = qkv_ref[rho, pl.ds(q_start, Q_BLOCK), base:base + LANES]
            k = qkv_ref[rho, pl.ds(k_start, K_BLOCK), base + LANES:base + 2 * LANES]
            v = qkv_ref[rho, pl.ds(k_start, K_BLOCK), base + 2 * LANES:base + 3 * LANES]
            zero = jnp.zeros_like(q)
            qq = jnp.concatenate([jnp.where(first_head, q, zero), jnp.where(first_head, zero, q)], axis=0)
            s = lax.dot_general(qq, k, (((1,), (1,)), ((), ())), preferred_element_type=F32)
            s = s + bias_ref[variant, pair]
            m = jnp.max(s, axis=1, keepdims=True)
            e = jnp.exp(s - m).astype(BF16)
            pv = jnp.dot(e, jnp.concatenate([v, ones], axis=1), preferred_element_type=F32)
            l = pv[:, LANES:]
            o2 = pv[:, :LANES] * pl.reciprocal(l, approx=True)
            o_pair = jnp.where(first_head, o2[:Q_BLOCK], o2[Q_BLOCK:])
            o_ref[rho, pl.ds(q_start, Q_BLOCK), pair * LANES:(pair + 1) * LANES] = o_pair.astype(BF16)
            lse = m + jnp.log(l)
            for h2 in range(2):
                lse_slab = jnp.where(lane_pair == 2 * pair + h2, lse[h2 * Q_BLOCK:(h2 + 1) * Q_BLOCK], lse_slab)
        head_part = lse_slab.astype(BF16).astype(F32)
        lse_slab = jnp.where((lane & 1) == 1, lse_slab - head_part, lse_slab)
        o_ref[rho, pl.ds(q_start, Q_BLOCK), GROUP_WIDTH:] = lse_slab.astype(BF16)
        return carry

    lax.fori_loop(0, dilation * n_blocks, block, 0, unroll=ATTN_UNROLL)


def _attention_group(qkv_g, group):
    batch = qkv_g.shape[0]
    dilation = DILATIONS[group]
    sub_len = SEQ // dilation
    kern = functools.partial(_attn_kernel, sub_len=sub_len, dilation=dilation, group=group)
    return pl.pallas_call(
        kern,
        grid=(batch,),
        in_specs=[pl.BlockSpec((None, dilation, sub_len, QKV_GROUP_WIDTH), lambda b: (b, 0, 0, 0))],
        out_specs=pl.BlockSpec((None, dilation, sub_len, O_GROUP_WIDTH), lambda b: (b, 0, 0, 0)),
        out_shape=jax.ShapeDtypeStruct((batch, dilation, sub_len, O_GROUP_WIDTH), BF16),
        scratch_shapes=[pltpu.VMEM((3, 2, 2 * Q_BLOCK, K_BLOCK), F32)],
        compiler_params=_params(1),
        name=f"dilated_attn_g{group}",
    )(qkv_g)


def _attn_out_kernel(x_ref, o0_ref, o1_ref, o2_ref, w_ref, out_ref, tok_ref):
    tm = x_ref.shape[0]
    n_slabs = O_GROUP_WIDTH // LANES
    slabs = []
    for group, (o_ref, dilation) in enumerate(zip((o0_ref, o1_ref, o2_ref), DILATIONS)):
        for s in range(n_slabs):
            cols = slice(s * LANES, (s + 1) * LANES)
            if dilation == 1:
                slabs.append(o_ref[0, :, cols].astype(F32))
            else:
                slot = (group - 1) * n_slabs + s
                for rho in range(dilation):
                    tok_ref[slot, pl.ds(rho, tm // dilation, stride=dilation), :] = o_ref[rho, :, cols].astype(F32)
                slabs.append(tok_ref[slot])
    lses = [slabs[g * n_slabs + n_slabs - 1] for g in range(N_GROUPS)]
    lses = [t + pltpu.roll(t, LANES - 1, 1) for t in lses]
    top = jnp.maximum(jnp.maximum(lses[0], lses[1]), lses[2])
    es = [jnp.exp(t - top) for t in lses]
    inv = 1.0 / (es[0] + es[1] + es[2])
    lane = lax.broadcasted_iota(jnp.int32, (tm, GROUP_WIDTH), 1)
    acc = x_ref[...]
    for g in range(N_GROUPS):
        alpha = es[g] * inv
        wide = jnp.where(lane < HEAD_DIM, alpha[:, 0:1],
                         jnp.where(lane < 2 * HEAD_DIM, alpha[:, 2:3],
                                   jnp.where(lane < 3 * HEAD_DIM, alpha[:, 4:5], alpha[:, 6:7])))
        o = jnp.concatenate(slabs[g * n_slabs:g * n_slabs + n_slabs - 1], axis=1)
        acc = acc + jnp.dot((o * wide).astype(BF16), w_ref[g * GROUP_WIDTH:(g + 1) * GROUP_WIDTH, :],
                            preferred_element_type=F32)
    out_ref[...] = acc


def _attn_out(x3, os, w_o):
    batch = x3.shape[0]
    n_strided_slabs = (N_GROUPS - 1) * (O_GROUP_WIDTH // LANES)
    return pl.pallas_call(
        _attn_out_kernel,
        grid=(batch, SEQ // TM_ROW),
        in_specs=[pl.BlockSpec((None, TM_ROW, D_MODEL), lambda b, i: (b, i, 0))]
                 + [_residue_major_spec(r, TM_ROW, O_GROUP_WIDTH) for r in DILATIONS]
                 + [_resident((ATTN_WIDTH, D_MODEL))],
        out_specs=pl.BlockSpec((None, TM_ROW, D_MODEL), lambda b, i: (b, i, 0)),
        out_shape=jax.ShapeDtypeStruct(x3.shape, F32),
        scratch_shapes=[pltpu.VMEM((n_strided_slabs, TM_ROW, LANES), F32)],
        compiler_params=_params(2),
        name="attn_out_proj",
    )(x3, *os, w_o)


def _gmlp_kernel(x_ref, g_ref, win_ref, gv_ref, ws_ref, bs_ref, wout_ref, out_ref, gated_ref):
    x = x_ref[...]
    h = _rms(x, g_ref[...]).astype(BF16)
    z = jax.nn.gelu(jnp.dot(h, win_ref[...], preferred_element_type=F32))
    u = z[:, :D_MODEL]
    v = _rms(z[:, D_MODEL:], gv_ref[...]).astype(BF16)
    for n in range(x.shape[0] // CHUNK):
        rows = slice(n * CHUNK, (n + 1) * CHUNK)
        for sg in range(N_SPATIAL_GROUPS):
            cols = slice(sg * CHUNK, (sg + 1) * CHUNK)
            s = jnp.dot(ws_ref[sg], v[rows, cols], preferred_element_type=F32) + bs_ref[sg]
            gated_ref[rows, cols] = (u[rows, cols] * s).astype(BF16)
    out_ref[...] = x + jnp.dot(gated_ref[...], wout_ref[...], preferred_element_type=F32)


def _gmlp(x2, g, w_in, g_v, w_s, b_s, w_out):
    n = x2.shape[0]
    return pl.pallas_call(
        _gmlp_kernel,
        grid=(n // TM_ROW,),
        in_specs=[pl.BlockSpec((TM_ROW, D_MODEL), lambda i: (i, 0)),
                  _resident((1, D_MODEL)),
                  _resident((D_MODEL, 2 * D_MODEL)),
                  _resident((1, D_MODEL)),
                  _resident((N_SPATIAL_GROUPS, CHUNK, CHUNK)),
                  _resident((N_SPATIAL_GROUPS, CHUNK, CHUNK)),
                  _resident((D_MODEL, D_MODEL))],
        out_specs=pl.BlockSpec((TM_ROW, D_MODEL), lambda i: (i, 0)),
        out_shape=jax.ShapeDtypeStruct((n, D_MODEL), F32),
        scratch_shapes=[pltpu.VMEM((TM_ROW, D_MODEL), BF16)],
        compiler_params=_params(1),
        name="chunk_gmlp",
    )(x2, g, w_in, g_v, w_s, b_s, w_out)


def _ffn_kernel(xm_ref, xp_ref, xn_ref, g_ref, wup_ref, cw_ref, cb_ref, wdn_ref, out_ref,
                acc_ref, h_ref, a_ref, c0_ref, c1_ref):
    i = pl.program_id(1)
    tm = xm_ref.shape[1]
    seg = tm // SUBLANES
    n_slabs = D_MODEL // LANES
    g = g_ref[...]
    streams = range(FFN_STREAMS)
    for u in streams:
        for s in range(SUBLANES):
            for slab in range(n_slabs):
                acc_ref[u, slab, pl.ds(s, seg, stride=SUBLANES), :] = (
                    xm_ref[u, s * seg:(s + 1) * seg, slab * LANES:(slab + 1) * LANES])
        xm = jnp.concatenate([acc_ref[u, slab] for slab in range(n_slabs)], axis=1)
        xp = jnp.where(i > 0, xp_ref[u], 0.0)
        xn = jnp.where(i < pl.num_programs(1) - 1, xn_ref[u], 0.0)
        h_ref[u] = jnp.concatenate([_rms(xp, g), _rms(xm, g), _rms(xn, g)], axis=0).astype(BF16)
    sublane = lax.broadcasted_iota(jnp.int32, (SUBLANES, 2 * FF_CHUNK), 0)

    def up(c, c_ref):
        cw = cw_ref[c]
        gate_col = pl.multiple_of(c * FF_CHUNK, FF_CHUNK)
        val_col = pl.multiple_of(D_FF + c * FF_CHUNK, LANES)
        for u in streams:
            a = a_ref.at[u]
            a[:, :FF_CHUNK] = jnp.dot(h_ref[u], wup_ref[:, pl.ds(gate_col, FF_CHUNK)], preferred_element_type=F32)
            a[:, FF_CHUNK:] = jnp.dot(h_ref[u], wup_ref[:, pl.ds(val_col, FF_CHUNK)], preferred_element_type=F32)
            a[0:HALO] = pltpu.roll(
                jnp.where(sublane == SUBLANES - 1, a[0:HALO], a[tm:tm + HALO]), 1, 0)
            a[tm + HALO:tm + 2 * HALO] = pltpu.roll(
                jnp.where(sublane == 0, a[tm + HALO:tm + 2 * HALO], a[HALO:2 * HALO]), SUBLANES - 1, 0)
            c_ref[u] = (cw[0:1] * a[0:tm, :] + cw[1:2] * a[HALO:HALO + tm, :]
                        + cw[2:3] * a[2 * HALO:2 * HALO + tm, :] + cb_ref[c])

    def down(c, c_ref):
        for u in streams:
            gate = c_ref[u, :, :FF_CHUNK]
            val = c_ref[u, :, FF_CHUNK:]
            act = gate * pl.reciprocal(1.0 + jnp.exp(-gate), approx=True) * val
            y = jnp.dot(act.astype(BF16), wdn_ref[pl.ds(pl.multiple_of(c * FF_CHUNK, FF_CHUNK), FF_CHUNK), :],
                        preferred_element_type=F32)
            for slab in range(n_slabs):
                acc_ref[u, slab] += y[:, slab * LANES:(slab + 1) * LANES]

    up(0, c0_ref)

    def two_chunks(k, carry):
        c = 2 * k
        up(c + 1, c1_ref)
        down(c, c0_ref)
        up(c + 2, c0_ref)
        down(c + 1, c1_ref)
        return carry

    assert N_FF_CHUNKS % 2 == 1
    lax.fori_loop(0, N_FF_CHUNKS // 2, two_chunks, 0)
    down(N_FF_CHUNKS - 1, c0_ref)

    for u in streams:
        for s in range(SUBLANES):
            for slab in range(n_slabs):
                out_ref[u, s * seg:(s + 1) * seg, slab * LANES:(slab + 1) * LANES] = (
                    acc_ref[u, slab, pl.ds(s, seg, stride=SUBLANES), :])


def _ffn(x3, g, w_up, conv_w, conv_b, w_down):
    batch = x3.shape[0]
    tiles = SEQ // TM
    halo_blocks = TM // HALO
    last_halo = SEQ // HALO - 1
    return pl.pallas_call(
        _ffn_kernel,
        grid=(batch // FFN_STREAMS, tiles),
        in_specs=[pl.BlockSpec((FFN_STREAMS, TM, D_MODEL), lambda b, i: (b, i, 0)),
                  pl.BlockSpec((FFN_STREAMS, HALO, D_MODEL),
                               lambda b, i: (b, jnp.maximum(i * halo_blocks - 1, 0), 0)),
                  pl.BlockSpec((FFN_STREAMS, HALO, D_MODEL),
                               lambda b, i: (b, jnp.minimum((i + 1) * halo_blocks, last_halo), 0)),
                  _resident((1, D_MODEL)),
                  _resident((D_MODEL, 2 * D_FF)),
                  _resident((N_FF_CHUNKS, 3, 2 * FF_CHUNK)),
                  _resident((N_FF_CHUNKS, 1, 2 * FF_CHUNK)),
                  _resident((D_FF, D_MODEL))],
        out_specs=pl.BlockSpec((FFN_STREAMS, TM, D_MODEL), lambda b, i: (b, i, 0)),
        out_shape=jax.ShapeDtypeStruct(x3.shape, F32),
        scratch_shapes=[pltpu.VMEM((FFN_STREAMS, D_MODEL // LANES, TM, LANES), F32),
                        pltpu.VMEM((FFN_STREAMS, TM + 2 * HALO, D_MODEL), BF16),
                        pltpu.VMEM((FFN_STREAMS, TM + 2 * HALO, 2 * FF_CHUNK), F32),
                        pltpu.VMEM((FFN_STREAMS, TM, 2 * FF_CHUNK), F32),
                        pltpu.VMEM((FFN_STREAMS, TM, 2 * FF_CHUNK), F32)],
        compiler_params=_params(2),
        name="conv_gated_ffn",
    )(x3, x3, x3, g, w_up, conv_w, conv_b, w_down)


def _qkv_column_order():
    cols = []
    for group in range(N_GROUPS):
        for pair in range(2):
            for part in range(3):
                for h2 in range(2):
                    head = group * HEADS_PER_GROUP + 2 * pair + h2
                    start = part * ATTN_WIDTH + head * HEAD_DIM
                    cols.extend(range(start, start + HEAD_DIM))
    return np.asarray(cols, dtype=np.int32)


def _chunk_gate_val(t):
    lead = t.shape[:-1]
    gv = t.reshape(lead + (2, N_FF_CHUNKS, FF_CHUNK))
    gv = jnp.moveaxis(gv, -2, 0)
    return gv.reshape((N_FF_CHUNKS,) + lead + (2 * FF_CHUNK,))


def _prep_ffn(layer, ffn_norm_g, ffn_w_up, ffn_conv_w, ffn_conv_b, ffn_w_down):
    return (ffn_norm_g[layer][None, :],
            ffn_w_up[layer].astype(BF16),
            _chunk_gate_val(ffn_conv_w[layer]),
            _chunk_gate_val(ffn_conv_b[layer][None, :]),
            ffn_w_down[layer].astype(BF16))


def _trunk(x, attn_w, gmlp_w, ffn_w0, ffn_w1):
    batch = x.shape[0]
    n = batch * SEQ

    norm_g, w_qkv, block_avg, qk_gain, w_o = attn_w
    qkvs = _qkv_proj(x, norm_g, w_qkv, block_avg, qk_gain)
    os = [_attention_group(qkv_g, group) for group, qkv_g in enumerate(qkvs)]
    x = _attn_out(x, os, w_o)
    x = _ffn(x, *ffn_w0)

    x2 = _gmlp(x.reshape(n, D_MODEL), *gmlp_w)
    return _ffn(x2.reshape(batch, SEQ, D_MODEL), *ffn_w1)


def kernel(x_prompt, x_sample, attn_norm_g, attn_w_qkv, attn_q_norm_g, attn_k_norm_g, attn_w_o, gmlp_norm_g, gmlp_w_in, gmlp_v_norm_g, gmlp_w_spatial, gmlp_b_spatial, gmlp_w_out, ffn_norm_g, ffn_w_up, ffn_conv_w, ffn_conv_b, ffn_w_down):
    head_of = np.arange(2 * LANES) // HEAD_DIM
    block_avg = jnp.asarray((head_of[:, None] == head_of[None, :]) / HEAD_DIM, dtype=BF16)
    qk_gain = jnp.concatenate([jnp.tile(attn_q_norm_g[0] * HEAD_DIM ** -0.5, 2),
                               jnp.tile(attn_k_norm_g[0], 2)])[None, :]
    attn_w = (attn_norm_g[0][None, :],
              attn_w_qkv[0][:, _qkv_column_order()].astype(BF16),
              block_avg, qk_gain,
              attn_w_o[0].astype(BF16))
    gmlp_w = (gmlp_norm_g[0][None, :],
              gmlp_w_in[0].astype(BF16),
              gmlp_v_norm_g[0][None, :],
              gmlp_w_spatial[0].astype(BF16),
              jnp.broadcast_to(gmlp_b_spatial[0][:, :, None], (N_SPATIAL_GROUPS, CHUNK, CHUNK)),
              gmlp_w_out[0].astype(BF16))
    ffn_w0 = _prep_ffn(0, ffn_norm_g, ffn_w_up, ffn_conv_w, ffn_conv_b, ffn_w_down)
    ffn_w1 = _prep_ffn(1, ffn_norm_g, ffn_w_up, ffn_conv_w, ffn_conv_b, ffn_w_down)
    return (_trunk(x_prompt, attn_w, gmlp_w, ffn_w0, ffn_w1),
            _trunk(x_sample, attn_w, gmlp_w, ffn_w0, ffn_w1))
```

```python
import functools

import numpy as np
import jax
import jax.numpy as jnp
from jax import lax
from jax.experimental import pallas as pl
from jax.experimental.pallas import tpu as pltpu

F32 = jnp.float32
BF16 = jnp.bfloat16

D_MODEL = 1024
SEQ = 4096
HEAD_DIM = 64
DILATIONS = (1, 4, 16)
N_GROUPS = len(DILATIONS)
HALF_WINDOW = 64
HEADS_PER_GROUP = 4
N_HEADS = N_GROUPS * HEADS_PER_GROUP
ATTN_WIDTH = N_HEADS * HEAD_DIM
GROUP_WIDTH = HEADS_PER_GROUP * HEAD_DIM
LANES = 128
PAIR_WIDTH = 2 * HEAD_DIM
assert PAIR_WIDTH == LANES
QKV_GROUP_WIDTH = 3 * GROUP_WIDTH
O_GROUP_WIDTH = GROUP_WIDTH + LANES
LSE_LANES = 2
CHUNK = 128
N_SPATIAL_GROUPS = 8
D_FF = 2816
EPS = 1e-6
NEG_INF = -1e30

Q_BLOCK = 128
K_BLOCK = Q_BLOCK + 2 * HALF_WINDOW
ATTN_UNROLL = 8
FF_CHUNK = 256
N_FF_CHUNKS = D_FF // FF_CHUNK
SUBLANES = 8
HALO = SUBLANES
TM = 512
TM_ROW = 1024
FFN_STREAMS = 2
VMEM_LIMIT = 56 * 1024 * 1024


def _rms(x, g):
    ms = jnp.mean(x * x, axis=-1, keepdims=True)
    return x * lax.rsqrt(ms + EPS) * g


def _params(n_axes, flags=None):
    return pltpu.CompilerParams(dimension_semantics=("arbitrary",) * n_axes,
                                vmem_limit_bytes=VMEM_LIMIT, flags=flags)


def _resident(shape):
    nd = len(shape)
    return pl.BlockSpec(shape, lambda *_: (0,) * nd, pipeline_mode=pl.Buffered(1))


def _residue_major_spec(dilation, rows, width):
    return pl.BlockSpec((None, dilation, rows // dilation, width), lambda b, i: (b, 0, i, 0))


def _qkv_kernel(x_ref, g_ref, w_ref, bd_ref, gv_ref, o0_ref, o1_ref, o2_ref, y_ref):
    tm = x_ref.shape[0]
    h = _rms(x_ref[...], g_ref[...]).astype(BF16)
    out_refs = (o0_ref, o1_ref, o2_ref)
    y_all = jnp.dot(h, w_ref[...], preferred_element_type=F32)
    for group, dilation in enumerate(DILATIONS):
        for pair in range(2):
            col0 = group * QKV_GROUP_WIDTH + pair * 3 * LANES
            y = y_all[:, col0:col0 + 3 * LANES]
            qk = y[:, :2 * LANES]
            ms = jnp.dot((qk * qk).astype(BF16), bd_ref[...], preferred_element_type=F32)
            qkn = qk * lax.rsqrt(ms + EPS) * gv_ref[...]
            slabs = (qkn[:, :LANES], qkn[:, LANES:], y[:, 2 * LANES:])
            out_ref = out_refs[group]
            for s, slab in enumerate(slabs):
                cols = slice((pair * 3 + s) * LANES, (pair * 3 + s + 1) * LANES)
                if dilation == 1:
                    out_ref[0, :, cols] = slab.astype(BF16)
                else:
                    slot = ((group - 1) * 2 + pair) * 3 + s
                    y_ref[slot] = slab
                    for rho in range(dilation):
                        rows = y_ref[slot, pl.ds(rho, tm // dilation, stride=dilation), :]
                        out_ref[rho, :, cols] = rows.astype(BF16)


def _qkv_proj(x3, g, w, bd, gv):
    batch = x3.shape[0]
    n_strided_slabs = (N_GROUPS - 1) * 2 * 3
    return pl.pallas_call(
        _qkv_kernel,
        grid=(batch, SEQ // TM_ROW),
        in_specs=[pl.BlockSpec((None, TM_ROW, D_MODEL), lambda b, i: (b, i, 0)),
                  _resident((1, D_MODEL)),
                  _resident((D_MODEL, 3 * ATTN_WIDTH)),
                  _resident((2 * LANES, 2 * LANES)),
                  _resident((1, 2 * LANES))],
        out_specs=[_residue_major_spec(r, TM_ROW, QKV_GROUP_WIDTH) for r in DILATIONS],
        out_shape=[jax.ShapeDtypeStruct((batch, r, SEQ // r, QKV_GROUP_WIDTH), BF16) for r in DILATIONS],
        scratch_shapes=[pltpu.VMEM((n_strided_slabs, TM_ROW, LANES), F32)],
        compiler_params=_params(2),
        name="qkv_proj",
    )(x3, g, w, bd, gv)


def _alibi_slope(head):
    return float(np.exp2(-8.0 * (head + 1) / N_HEADS))


def _attn_kernel(qkv_ref, o_ref, bias_ref, *, sub_len, dilation, group):
    n_blocks = sub_len // Q_BLOCK
    assert n_blocks & (n_blocks - 1) == 0
    offsets = (0, HALF_WINDOW, 2 * HALF_WINDOW)

    @pl.when(pl.program_id(0) == 0)
    def _build_bias():
        row = lax.broadcasted_iota(jnp.int32, (2 * Q_BLOCK, K_BLOCK), 0)
        key = lax.broadcasted_iota(jnp.int32, (2 * Q_BLOCK, K_BLOCK), 1)
        query = row & (Q_BLOCK - 1)
        for variant, off in enumerate(offsets):
            rel = jnp.abs(key - off - query)
            dist = (dilation * rel).astype(F32)
            for pair in range(2):
                head = group * HEADS_PER_GROUP + 2 * pair
                slope = jnp.where(row < Q_BLOCK, _alibi_slope(head), _alibi_slope(head + 1))
                bias_ref[variant, pair] = jnp.where(rel <= HALF_WINDOW, -(slope * dist), NEG_INF)

    lane = lax.broadcasted_iota(jnp.int32, (Q_BLOCK, LANES), 1)
    first_head = lane < HEAD_DIM
    lane_head = lane >> (LSE_LANES.bit_length() - 1)
    ones = jnp.ones((K_BLOCK, LANES), BF16)

    def block(unit, carry):
        rho = lax.shift_right_logical(unit, n_blocks.bit_length() - 1)
        n = unit & (n_blocks - 1)
        q_start = pl.multiple_of(n * Q_BLOCK, Q_BLOCK)
        k_start = pl.multiple_of(jnp.clip(n * Q_BLOCK - HALF_WINDOW, 0, sub_len - K_BLOCK), HALF_WINDOW)
        variant = jnp.where(n == 0, 0, jnp.where(n == n_blocks - 1, 2, 1))
        lse_slab = jnp.zeros((Q_BLOCK, LANES), F32)
        for pair in range(2):
            base = pair * 3 * LANES
            q = qkv_ref[rho, pl.ds(q_start, Q_BLOCK), base:base + LANES]
            k = qkv_ref[rho, pl.ds(k_start, K_BLOCK), base + LANES:base + 2 * LANES]
            v = qkv_ref[rho, pl.ds(k_start, K_BLOCK), base + 2 * LANES:base + 3 * LANES]
            zero = jnp.zeros_like(q)
            qq = jnp.concatenate([jnp.where(first_head, q, zero), jnp.where(first_head, zero, q)], axis=0)
            s = lax.dot_general(qq, k, (((1,), (1,)), ((), ())), preferred_element_type=F32)
            s = s + bias_ref[variant, pair]
            m = jnp.max(s, axis=1, keepdims=True)
            e = jnp.exp(s - m).astype(BF16)
            pv = jnp.dot(e, jnp.concatenate([v, ones], axis=1), preferred_element_type=F32)
            l = pv[:, LANES:]
            o2 = pv[:, :LANES] * pl.reciprocal(l, approx=True)
            o_pair = jnp.where(first_head, o2[:Q_BLOCK], o2[Q_BLOCK:])
            o_ref[rho, pl.ds(q_start, Q_BLOCK), pair * LANES:(pair + 1) * LANES] = o_pair.astype(BF16)
            lse = m + jnp.log(l)
            for h2 in range(2):
                lse_slab = jnp.where(lane_head == 2 * pair + h2, lse[h2 * Q_BLOCK:(h2 + 1) * Q_BLOCK], lse_slab)
        head_part = lse_slab.astype(BF16).astype(F32)
        lse_slab = jnp.where((lane & 1) == 1, lse_slab - head_part, lse_slab)
        o_ref[rho, pl.ds(q_start, Q_BLOCK), GROUP_WIDTH:] = lse_slab.astype(BF16)
        return carry

    lax.fori_loop(0, dilation * n_blocks, block, 0, unroll=ATTN_UNROLL)


def _attention_group(qkv_g, group):
    batch = qkv_g.shape[0]
    dilation = DILATIONS[group]
    sub_len = SEQ // dilation
    kern = functools.partial(_attn_kernel, sub_len=sub_len, dilation=dilation, group=group)
    return pl.pallas_call(
        kern,
        grid=(batch,),
        in_specs=[pl.BlockSpec((None, dilation, sub_len, QKV_GROUP_WIDTH), lambda b: (b, 0, 0, 0))],
        out_specs=pl.BlockSpec((None, dilation, sub_len, O_GROUP_WIDTH), lambda b: (b, 0, 0, 0)),
        out_shape=jax.ShapeDtypeStruct((batch, dilation, sub_len, O_GROUP_WIDTH), BF16),
        scratch_shapes=[pltpu.VMEM((3, 2, 2 * Q_BLOCK, K_BLOCK), F32)],
        compiler_params=_params(1),
        name=f"dilated_attn_g{group}",
    )(qkv_g)


def _attn_out_kernel(x_ref, o0_ref, o1_ref, o2_ref, w_ref, out_ref, tok_ref):
    tm = x_ref.shape[0]
    n_slabs = O_GROUP_WIDTH // LANES
    slabs = []
    for group, (o_ref, dilation) in enumerate(zip((o0_ref, o1_ref, o2_ref), DILATIONS)):
        for s in range(n_slabs):
            cols = slice(s * LANES, (s + 1) * LANES)
            if dilation == 1:
                slabs.append(o_ref[0, :, cols].astype(F32))
            else:
                slot = (group - 1) * n_slabs + s
                for rho in range(dilation):
                    tok_ref[slot, pl.ds(rho, tm // dilation, stride=dilation), :] = o_ref[rho, :, cols].astype(F32)
                slabs.append(tok_ref[slot])
    lses = [slabs[g * n_slabs + n_slabs - 1] for g in range(N_GROUPS)]
    lses = [t + pltpu.roll(t, LANES - 1, 1) for t in lses]
    top = jnp.maximum(jnp.maximum(lses[0], lses[1]), lses[2])
    es = [jnp.exp(t - top) for t in lses]
    inv = 1.0 / (es[0] + es[1] + es[2])
    lane = lax.broadcasted_iota(jnp.int32, (tm, GROUP_WIDTH), 1)
    acc = x_ref[...]
    for g in range(N_GROUPS):
        alpha = es[g] * inv
        col = [alpha[:, LSE_LANES * hg:LSE_LANES * hg + 1] for hg in range(HEADS_PER_GROUP)]
        wide = jnp.where(lane < HEAD_DIM, col[0],
                         jnp.where(lane < 2 * HEAD_DIM, col[1], jnp.where(lane < 3 * HEAD_DIM, col[2], col[3])))
        o = jnp.concatenate(slabs[g * n_slabs:g * n_slabs + n_slabs - 1], axis=1)
        acc = acc + jnp.dot((o * wide).astype(BF16), w_ref[g * GROUP_WIDTH:(g + 1) * GROUP_WIDTH, :],
                            preferred_element_type=F32)
    out_ref[...] = acc


def _attn_out(x3, os, w_o):
    batch = x3.shape[0]
    n_strided_slabs = (N_GROUPS - 1) * (O_GROUP_WIDTH // LANES)
    return pl.pallas_call(
        _attn_out_kernel,
        grid=(batch, SEQ // TM_ROW),
        in_specs=[pl.BlockSpec((None, TM_ROW, D_MODEL), lambda b, i: (b, i, 0))]
                 + [_residue_major_spec(r, TM_ROW, O_GROUP_WIDTH) for r in DILATIONS]
                 + [_resident((ATTN_WIDTH, D_MODEL))],
        out_specs=pl.BlockSpec((None, TM_ROW, D_MODEL), lambda b, i: (b, i, 0)),
        out_shape=jax.ShapeDtypeStruct(x3.shape, F32),
        scratch_shapes=[pltpu.VMEM((n_strided_slabs, TM_ROW, LANES), F32)],
        compiler_params=_params(2),
        name="attn_out_proj",
    )(x3, *os, w_o)


def _gmlp_kernel(x_ref, g_ref, win_ref, gv_ref, ws_ref, bs_ref, wout_ref, out_ref, gated_ref):
    x = x_ref[...]
    h = _rms(x, g_ref[...]).astype(BF16)
    z = jax.nn.gelu(jnp.dot(h, win_ref[...], preferred_element_type=F32))
    u = z[:, :D_MODEL]
    v = _rms(z[:, D_MODEL:], gv_ref[...]).astype(BF16)
    for n in range(x.shape[0] // CHUNK):
        rows = slice(n * CHUNK, (n + 1) * CHUNK)
        for sg in range(N_SPATIAL_GROUPS):
            cols = slice(sg * CHUNK, (sg + 1) * CHUNK)
            s = jnp.dot(ws_ref[sg], v[rows, cols], preferred_element_type=F32) + bs_ref[sg]
            gated_ref[rows, cols] = (u[rows, cols] * s).astype(BF16)
    out_ref[...] = x + jnp.dot(gated_ref[...], wout_ref[...], preferred_element_type=F32)


def _gmlp(x2, g, w_in, g_v, w_s, b_s, w_out):
    n = x2.shape[0]
    return pl.pallas_call(
        _gmlp_kernel,
        grid=(n // TM_ROW,),
        in_specs=[pl.BlockSpec((TM_ROW, D_MODEL), lambda i: (i, 0)),
                  _resident((1, D_MODEL)),
                  _resident((D_MODEL, 2 * D_MODEL)),
                  _resident((1, D_MODEL)),
                  _resident((N_SPATIAL_GROUPS, CHUNK, CHUNK)),
                  _resident((N_SPATIAL_GROUPS, CHUNK, CHUNK)),
                  _resident((D_MODEL, D_MODEL))],
        out_specs=pl.BlockSpec((TM_ROW, D_MODEL), lambda i: (i, 0)),
        out_shape=jax.ShapeDtypeStruct((n, D_MODEL), F32),
        scratch_shapes=[pltpu.VMEM((TM_ROW, D_MODEL), BF16)],
        compiler_params=_params(1),
        name="chunk_gmlp",
    )(x2, g, w_in, g_v, w_s, b_s, w_out)


def _ffn_kernel(xm_ref, xp_ref, xn_ref, g_ref, wup_ref, cw_ref, cb_ref, wdn_ref, out_ref,
                acc_ref, h_ref, a_ref, c0_ref, c1_ref):
    i = pl.program_id(1)
    tm = xm_ref.shape[1]
    seg = tm // SUBLANES
    n_slabs = D_MODEL // LANES
    g = g_ref[...]
    c_bufs = (c0_ref, c1_ref)
    sublane = lax.broadcasted_iota(jnp.int32, (SUBLANES, 2 * FF_CHUNK), 0)

    def load_tile(u):
        for s in range(SUBLANES):
            for slab in range(n_slabs):
                acc_ref[u, slab, pl.ds(s, seg, stride=SUBLANES), :] = (
                    xm_ref[u, s * seg:(s + 1) * seg, slab * LANES:(slab + 1) * LANES])
        xm = jnp.concatenate([acc_ref[u, slab] for slab in range(n_slabs)], axis=1)
        xp = jnp.where(i > 0, xp_ref[u], 0.0)
        xn = jnp.where(i < pl.num_programs(1) - 1, xn_ref[u], 0.0)
        h_ref[u] = jnp.concatenate([_rms(xp, g), _rms(xm, g), _rms(xn, g)], axis=0).astype(BF16)

    def store_tile(u):
        for s in range(SUBLANES):
            for slab in range(n_slabs):
                out_ref[u, s * seg:(s + 1) * seg, slab * LANES:(slab + 1) * LANES] = (
                    acc_ref[u, slab, pl.ds(s, seg, stride=SUBLANES), :])

    def up(u, c, parity):
        cw = cw_ref[c]
        gate_col = pl.multiple_of(c * FF_CHUNK, FF_CHUNK)
        val_col = pl.multiple_of(D_FF + c * FF_CHUNK, LANES)
        a = a_ref.at[u]
        a[:, :FF_CHUNK] = jnp.dot(h_ref[u], wup_ref[:, pl.ds(gate_col, FF_CHUNK)], preferred_element_type=F32)
        a[:, FF_CHUNK:] = jnp.dot(h_ref[u], wup_ref[:, pl.ds(val_col, FF_CHUNK)], preferred_element_type=F32)
        a[0:HALO] = pltpu.roll(
            jnp.where(sublane == SUBLANES - 1, a[0:HALO], a[tm:tm + HALO]), 1, 0)
        a[tm + HALO:tm + 2 * HALO] = pltpu.roll(
            jnp.where(sublane == 0, a[tm + HALO:tm + 2 * HALO], a[HALO:2 * HALO]), SUBLANES - 1, 0)
        c_bufs[parity][u] = (cw[0:1] * a[0:tm, :] + cw[1:2] * a[HALO:HALO + tm, :]
                             + cw[2:3] * a[2 * HALO:2 * HALO + tm, :] + cb_ref[c])

    def down(u, c, parity):
        gate = c_bufs[parity][u, :, :FF_CHUNK]
        val = c_bufs[parity][u, :, FF_CHUNK:]
        act = gate * pl.reciprocal(1.0 + jnp.exp(-gate), approx=True) * val
        y = jnp.dot(act.astype(BF16), wdn_ref[pl.ds(pl.multiple_of(c * FF_CHUNK, FF_CHUNK), FF_CHUNK), :],
                    preferred_element_type=F32)
        for slab in range(n_slabs):
            acc_ref[u, slab] += y[:, slab * LANES:(slab + 1) * LANES]

    streams = range(FFN_STREAMS)
    for u in streams:
        load_tile(u)
    for u in streams:
        up(u, 0, 0)

    def two_chunks(k, carry):
        c = 2 * k
        for u in streams:
            up(u, c + 1, 1)
        for u in streams:
            down(u, c, 0)
        for u in streams:
            up(u, c + 2, 0)
        for u in streams:
            down(u, c + 1, 1)
        return carry

    assert N_FF_CHUNKS % 2 == 1
    lax.fori_loop(0, N_FF_CHUNKS // 2, two_chunks, 0)
    for u in streams:
        down(u, N_FF_CHUNKS - 1, 0)
    for u in streams:
        store_tile(u)


def _ffn(x3, g, w_up, conv_w, conv_b, w_down):
    batch = x3.shape[0]
    tiles = SEQ // TM
    halo_blocks = TM // HALO
    last_halo = SEQ // HALO - 1
    return pl.pallas_call(
        _ffn_kernel,
        grid=(batch // FFN_STREAMS, tiles),
        in_specs=[pl.BlockSpec((FFN_STREAMS, TM, D_MODEL), lambda b, i: (b, i, 0)),
                  pl.BlockSpec((FFN_STREAMS, HALO, D_MODEL),
                               lambda b, i: (b, jnp.maximum(i * halo_blocks - 1, 0), 0)),
                  pl.BlockSpec((FFN_STREAMS, HALO, D_MODEL),
                               lambda b, i: (b, jnp.minimum((i + 1) * halo_blocks, last_halo), 0)),
                  _resident((1, D_MODEL)),
                  _resident((D_MODEL, 2 * D_FF)),
                  _resident((N_FF_CHUNKS, 3, 2 * FF_CHUNK)),
                  _resident((N_FF_CHUNKS, 1, 2 * FF_CHUNK)),
                  _resident((D_FF, D_MODEL))],
        out_specs=pl.BlockSpec((FFN_STREAMS, TM, D_MODEL), lambda b, i: (b, i, 0)),
        out_shape=jax.ShapeDtypeStruct(x3.shape, F32),
        scratch_shapes=[pltpu.VMEM((FFN_STREAMS, D_MODEL // LANES, TM, LANES), F32),
                        pltpu.VMEM((FFN_STREAMS, TM + 2 * HALO, D_MODEL), BF16),
                        pltpu.VMEM((FFN_STREAMS, TM + 2 * HALO, 2 * FF_CHUNK), F32),
                        pltpu.VMEM((FFN_STREAMS, TM, 2 * FF_CHUNK), F32),
                        pltpu.VMEM((FFN_STREAMS, TM, 2 * FF_CHUNK), F32)],
        compiler_params=_params(2),
        name="conv_gated_ffn",
    )(x3, x3, x3, g, w_up, conv_w, conv_b, w_down)


def _qkv_column_order():
    cols = []
    for group in range(N_GROUPS):
        for pair in range(2):
            for part in range(3):
                for h2 in range(2):
                    head = group * HEADS_PER_GROUP + 2 * pair + h2
                    start = part * ATTN_WIDTH + head * HEAD_DIM
                    cols.extend(range(start, start + HEAD_DIM))
    return np.asarray(cols, dtype=np.int32)


def _chunk_gate_val(t):
    lead = t.shape[:-1]
    gv = t.reshape(lead + (2, N_FF_CHUNKS, FF_CHUNK))
    gv = jnp.moveaxis(gv, -2, 0)
    return gv.reshape((N_FF_CHUNKS,) + lead + (2 * FF_CHUNK,))


def _prep_ffn(layer, ffn_norm_g, ffn_w_up, ffn_conv_w, ffn_conv_b, ffn_w_down):
    return (ffn_norm_g[layer][None, :],
            ffn_w_up[layer].astype(BF16),
            _chunk_gate_val(ffn_conv_w[layer]),
            _chunk_gate_val(ffn_conv_b[layer][None, :]),
            ffn_w_down[layer].astype(BF16))


def _trunk(x, attn_w, gmlp_w, ffn_w0, ffn_w1):
    batch = x.shape[0]
    n = batch * SEQ

    norm_g, w_qkv, block_avg, qk_gain, w_o = attn_w
    qkvs = _qkv_proj(x, norm_g, w_qkv, block_avg, qk_gain)
    os = [_attention_group(qkv_g, group) for group, qkv_g in enumerate(qkvs)]
    x = _attn_out(x, os, w_o)
    x = _ffn(x, *ffn_w0)

    x2 = _gmlp(x.reshape(n, D_MODEL), *gmlp_w)
    return _ffn(x2.reshape(batch, SEQ, D_MODEL), *ffn_w1)


def kernel(x_prompt, x_sample, attn_norm_g, attn_w_qkv, attn_q_norm_g, attn_k_norm_g, attn_w_o, gmlp_norm_g, gmlp_w_in, gmlp_v_norm_g, gmlp_w_spatial, gmlp_b_spatial, gmlp_w_out, ffn_norm_g, ffn_w_up, ffn_conv_w, ffn_conv_b, ffn_w_down):
    head_of = np.arange(2 * LANES) // HEAD_DIM
    block_avg = jnp.asarray((head_of[:, None] == head_of[None, :]) / HEAD_DIM, dtype=BF16)
    qk_gain = jnp.concatenate([jnp.tile(attn_q_norm_g[0] * HEAD_DIM ** -0.5, 2),
                               jnp.tile(attn_k_norm_g[0], 2)])[None, :]
    attn_w = (attn_norm_g[0][None, :],
              attn_w_qkv[0][:, _qkv_column_order()].astype(BF16),
              block_avg, qk_gain,
              attn_w_o[0].astype(BF16))
    gmlp_w = (gmlp_norm_g[0][None, :],
              gmlp_w_in[0].astype(BF16),
              gmlp_v_norm_g[0][None, :],
              gmlp_w_spatial[0].astype(BF16),
              jnp.broadcast_to(gmlp_b_spatial[0][:, :, None], (N_SPATIAL_GROUPS, CHUNK, CHUNK)),
              gmlp_w_out[0].astype(BF16))
    ffn_w0 = _prep_ffn(0, ffn_norm_g, ffn_w_up, ffn_conv_w, ffn_conv_b, ffn_w_down)
    ffn_w1 = _prep_ffn(1, ffn_norm_g, ffn_w_up, ffn_conv_w, ffn_conv_b, ffn_w_down)
    return (_trunk(x_prompt, attn_w, gmlp_w, ffn_w0, ffn_w1),
            _trunk(x_sample, attn_w, gmlp_w, ffn_w0, ffn_w1))
```

```python
import functools

import numpy as np
import jax
import jax.numpy as jnp
from jax import lax
from jax.experimental import pallas as pl
from jax.experimental.pallas import tpu as pltpu

F32 = jnp.float32
BF16 = jnp.bfloat16

D_MODEL = 1024
SEQ = 4096
HEAD_DIM = 64
DILATIONS = (1, 4, 16)
N_GROUPS = len(DILATIONS)
HALF_WINDOW = 64
HEADS_PER_GROUP = 4
N_HEADS = N_GROUPS * HEADS_PER_GROUP
ATTN_WIDTH = N_HEADS * HEAD_DIM
GROUP_WIDTH = HEADS_PER_GROUP * HEAD_DIM
LANES = 128
PAIR_WIDTH = 2 * HEAD_DIM
assert PAIR_WIDTH == LANES
QKV_GROUP_WIDTH = 3 * GROUP_WIDTH
QKV_COLUMN_GROUPS = (2, 1, 0)
O_GROUP_WIDTH = GROUP_WIDTH + LANES
LSE_LANES = 2
CHUNK = 128
N_SPATIAL_GROUPS = 8
D_FF = 2816
EPS = 1e-6
NEG_INF = -1e30

Q_BLOCK = 128
K_BLOCK = Q_BLOCK + 2 * HALF_WINDOW
ATTN_UNROLL = 8
FF_CHUNK = 256
N_FF_CHUNKS = D_FF // FF_CHUNK
SUBLANES = 8
HALO = SUBLANES
TM = 512
TM_ROW = 1024
FFN_STREAMS = 2
VMEM_LIMIT = 56 * 1024 * 1024


def _rms(x, g):
    ms = jnp.mean(x * x, axis=-1, keepdims=True)
    return x * lax.rsqrt(ms + EPS) * g


def _params(n_axes):
    return pltpu.CompilerParams(dimension_semantics=("arbitrary",) * n_axes,
                                vmem_limit_bytes=VMEM_LIMIT)


def _resident(shape):
    nd = len(shape)
    return pl.BlockSpec(shape, lambda *_: (0,) * nd, pipeline_mode=pl.Buffered(1))


def _residue_major_spec(dilation, rows, width):
    return pl.BlockSpec((None, dilation, rows // dilation, width), lambda b, i: (b, 0, i, 0))


def _qkv_kernel(x_ref, g_ref, w_ref, bd_ref, gv_ref, o0_ref, o1_ref, o2_ref, y_ref):
    tm = x_ref.shape[0]
    h = _rms(x_ref[...], g_ref[...]).astype(BF16)
    out_refs = (o0_ref, o1_ref, o2_ref)
    y_all = jnp.dot(h, w_ref[...], preferred_element_type=F32)
    for position, group in enumerate(QKV_COLUMN_GROUPS):
        dilation = DILATIONS[group]
        for pair in range(2):
            col0 = position * QKV_GROUP_WIDTH + pair * 3 * LANES
            y = y_all[:, col0:col0 + 3 * LANES]
            qk = y[:, :2 * LANES]
            ms = jnp.dot((qk * qk).astype(BF16), bd_ref[...], preferred_element_type=F32)
            qkn = qk * lax.rsqrt(ms + EPS) * gv_ref[...]
            slabs = (qkn[:, :LANES], qkn[:, LANES:], y[:, 2 * LANES:])
            out_ref = out_refs[group]
            for s, slab in enumerate(slabs):
                cols = slice((pair * 3 + s) * LANES, (pair * 3 + s + 1) * LANES)
                if dilation == 1:
                    out_ref[0, :, cols] = slab.astype(BF16)
                else:
                    slot = ((group - 1) * 2 + pair) * 3 + s
                    y_ref[slot] = slab
                    for rho in range(dilation):
                        rows = y_ref[slot, pl.ds(rho, tm // dilation, stride=dilation), :]
                        out_ref[rho, :, cols] = rows.astype(BF16)


def _qkv_proj(x3, g, w, bd, gv):
    batch = x3.shape[0]
    n_strided_slabs = (N_GROUPS - 1) * 2 * 3
    return pl.pallas_call(
        _qkv_kernel,
        grid=(batch, SEQ // TM_ROW),
        in_specs=[pl.BlockSpec((None, TM_ROW, D_MODEL), lambda b, i: (b, i, 0)),
                  _resident((1, D_MODEL)),
                  _resident((D_MODEL, 3 * ATTN_WIDTH)),
                  _resident((2 * LANES, 2 * LANES)),
                  _resident((1, 2 * LANES))],
        out_specs=[_residue_major_spec(r, TM_ROW, QKV_GROUP_WIDTH) for r in DILATIONS],
        out_shape=[jax.ShapeDtypeStruct((batch, r, SEQ // r, QKV_GROUP_WIDTH), BF16) for r in DILATIONS],
        scratch_shapes=[pltpu.VMEM((n_strided_slabs, TM_ROW, LANES), F32)],
        compiler_params=_params(2),
        name="qkv_proj",
    )(x3, g, w, bd, gv)


def _alibi_slope(head):
    return float(np.exp2(-8.0 * (head + 1) / N_HEADS))


def _attn_kernel(qkv_ref, o_ref, bias_ref, *, sub_len, dilation, group):
    n_blocks = sub_len // Q_BLOCK
    assert n_blocks & (n_blocks - 1) == 0
    offsets = (0, HALF_WINDOW, 2 * HALF_WINDOW)

    @pl.when(pl.program_id(0) == 0)
    def _build_bias():
        row = lax.broadcasted_iota(jnp.int32, (2 * Q_BLOCK, K_BLOCK), 0)
        key = lax.broadcasted_iota(jnp.int32, (2 * Q_BLOCK, K_BLOCK), 1)
        query = row & (Q_BLOCK - 1)
        for variant, off in enumerate(offsets):
            rel = jnp.abs(key - off - query)
            dist = (dilation * rel).astype(F32)
            for pair in range(2):
                head = group * HEADS_PER_GROUP + 2 * pair
                slope = jnp.where(row < Q_BLOCK, _alibi_slope(head), _alibi_slope(head + 1))
                bias_ref[variant, pair] = jnp.where(rel <= HALF_WINDOW, -(slope * dist), NEG_INF)

    lane = lax.broadcasted_iota(jnp.int32, (Q_BLOCK, LANES), 1)
    first_head = lane < HEAD_DIM
    lane_head = lane >> (LSE_LANES.bit_length() - 1)
    ones = jnp.ones((K_BLOCK, LANES), BF16)

    def block(unit, carry):
        rho = lax.shift_right_logical(unit, n_blocks.bit_length() - 1)
        n = unit & (n_blocks - 1)
        q_start = pl.multiple_of(n * Q_BLOCK, Q_BLOCK)
        k_start = pl.multiple_of(jnp.clip(n * Q_BLOCK - HALF_WINDOW, 0, sub_len - K_BLOCK), HALF_WINDOW)
        variant = jnp.where(n == 0, 0, jnp.where(n == n_blocks - 1, 2, 1))
        lse_slab = jnp.zeros((Q_BLOCK, LANES), F32)
        for pair in range(2):
            base = pair * 3 * LANES
            q = qkv_ref[rho, pl.ds(q_start, Q_BLOCK), base:base + LANES]
            k = qkv_ref[rho, pl.ds(k_start, K_BLOCK), base + LANES:base + 2 * LANES]
            v = qkv_ref[rho, pl.ds(k_start, K_BLOCK), base + 2 * LANES:base + 3 * LANES]
            zero = jnp.zeros_like(q)
            qq = jnp.concatenate([jnp.where(first_head, q, zero), jnp.where(first_head, zero, q)], axis=0)
            s = lax.dot_general(qq, k, (((1,), (1,)), ((), ())), preferred_element_type=F32)
            s = s + bias_ref[variant, pair]
            m = jnp.max(s, axis=1, keepdims=True)
            e = jnp.exp(s - m).astype(BF16)
            pv = jnp.dot(e, jnp.concatenate([v, ones], axis=1), preferred_element_type=F32)
            l = pv[:, LANES:]
            o2 = pv[:, :LANES] * pl.reciprocal(l, approx=True)
            o_pair = jnp.where(first_head, o2[:Q_BLOCK], o2[Q_BLOCK:])
            o_ref[rho, pl.ds(q_start, Q_BLOCK), pair * LANES:(pair + 1) * LANES] = o_pair.astype(BF16)
            lse = m + jnp.log(l)
            for h2 in range(2):
                lse_slab = jnp.where(lane_head == 2 * pair + h2, lse[h2 * Q_BLOCK:(h2 + 1) * Q_BLOCK], lse_slab)
        head_part = lse_slab.astype(BF16).astype(F32)
        lse_slab = jnp.where((lane & 1) == 1, lse_slab - head_part, lse_slab)
        o_ref[rho, pl.ds(q_start, Q_BLOCK), GROUP_WIDTH:] = lse_slab.astype(BF16)
        return carry

    lax.fori_loop(0, dilation * n_blocks, block, 0, unroll=ATTN_UNROLL)


def _attention_group(qkv_g, group):
    batch = qkv_g.shape[0]
    dilation = DILATIONS[group]
    sub_len = SEQ // dilation
    kern = functools.partial(_attn_kernel, sub_len=sub_len, dilation=dilation, group=group)
    return pl.pallas_call(
        kern,
        grid=(batch,),
        in_specs=[pl.BlockSpec((None, dilation, sub_len, QKV_GROUP_WIDTH), lambda b: (b, 0, 0, 0))],
        out_specs=pl.BlockSpec((None, dilation, sub_len, O_GROUP_WIDTH), lambda b: (b, 0, 0, 0)),
        out_shape=jax.ShapeDtypeStruct((batch, dilation, sub_len, O_GROUP_WIDTH), BF16),
        scratch_shapes=[pltpu.VMEM((3, 2, 2 * Q_BLOCK, K_BLOCK), F32)],
        compiler_params=_params(1),
        name=f"dilated_attn_g{group}",
    )(qkv_g)


def _attn_out_kernel(x_ref, o0_ref, o1_ref, o2_ref, w_ref, out_ref, tok_ref):
    tm = x_ref.shape[0]
    n_slabs = O_GROUP_WIDTH // LANES
    slabs = []
    for group, (o_ref, dilation) in enumerate(zip((o0_ref, o1_ref, o2_ref), DILATIONS)):
        for s in range(n_slabs):
            cols = slice(s * LANES, (s + 1) * LANES)
            if dilation == 1:
                slabs.append(o_ref[0, :, cols].astype(F32))
            else:
                slot = (group - 1) * n_slabs + s
                for rho in range(dilation):
                    tok_ref[slot, pl.ds(rho, tm // dilation, stride=dilation), :] = o_ref[rho, :, cols].astype(F32)
                slabs.append(tok_ref[slot])
    lses = [slabs[g * n_slabs + n_slabs - 1] for g in range(N_GROUPS)]
    lses = [t + pltpu.roll(t, LANES - 1, 1) for t in lses]
    top = jnp.maximum(jnp.maximum(lses[0], lses[1]), lses[2])
    es = [jnp.exp(t - top) for t in lses]
    inv = 1.0 / (es[0] + es[1] + es[2])
    lane = lax.broadcasted_iota(jnp.int32, (tm, GROUP_WIDTH), 1)
    acc = x_ref[...]
    for g in range(N_GROUPS):
        alpha = es[g] * inv
        col = [alpha[:, LSE_LANES * hg:LSE_LANES * hg + 1] for hg in range(HEADS_PER_GROUP)]
        wide = jnp.where(lane < HEAD_DIM, col[0],
                         jnp.where(lane < 2 * HEAD_DIM, col[1], jnp.where(lane < 3 * HEAD_DIM, col[2], col[3])))
        o = jnp.concatenate(slabs[g * n_slabs:g * n_slabs + n_slabs - 1], axis=1)
        acc = acc + jnp.dot((o * wide).astype(BF16), w_ref[g * GROUP_WIDTH:(g + 1) * GROUP_WIDTH, :],
                            preferred_element_type=F32)
    out_ref[...] = acc


def _attn_out(x3, os, w_o):
    batch = x3.shape[0]
    n_strided_slabs = (N_GROUPS - 1) * (O_GROUP_WIDTH // LANES)
    return pl.pallas_call(
        _attn_out_kernel,
        grid=(batch, SEQ // TM_ROW),
        in_specs=[pl.BlockSpec((None, TM_ROW, D_MODEL), lambda b, i: (b, i, 0))]
                 + [_residue_major_spec(r, TM_ROW, O_GROUP_WIDTH) for r in DILATIONS]
                 + [_resident((ATTN_WIDTH, D_MODEL))],
        out_specs=pl.BlockSpec((None, TM_ROW, D_MODEL), lambda b, i: (b, i, 0)),
        out_shape=jax.ShapeDtypeStruct(x3.shape, F32),
        scratch_shapes=[pltpu.VMEM((n_strided_slabs, TM_ROW, LANES), F32)],
        compiler_params=_params(2),
        name="attn_out_proj",
    )(x3, *os, w_o)


def _gmlp_kernel(x_ref, g_ref, win_ref, gv_ref, ws_ref, bs_ref, wout_ref, out_ref, gated_ref):
    x = x_ref[...]
    h = _rms(x, g_ref[...]).astype(BF16)
    z = jax.nn.gelu(jnp.dot(h, win_ref[...], preferred_element_type=F32))
    u = z[:, :D_MODEL]
    v = _rms(z[:, D_MODEL:], gv_ref[...]).astype(BF16)
    for n in range(x.shape[0] // CHUNK):
        rows = slice(n * CHUNK, (n + 1) * CHUNK)
        for sg in range(N_SPATIAL_GROUPS):
            cols = slice(sg * CHUNK, (sg + 1) * CHUNK)
            s = jnp.dot(ws_ref[sg], v[rows, cols], preferred_element_type=F32) + bs_ref[sg]
            gated_ref[rows, cols] = (u[rows, cols] * s).astype(BF16)
    out_ref[...] = x + jnp.dot(gated_ref[...], wout_ref[...], preferred_element_type=F32)


def _gmlp(x2, g, w_in, g_v, w_s, b_s, w_out):
    n = x2.shape[0]
    return pl.pallas_call(
        _gmlp_kernel,
        grid=(n // TM_ROW,),
        in_specs=[pl.BlockSpec((TM_ROW, D_MODEL), lambda i: (i, 0)),
                  _resident((1, D_MODEL)),
                  _resident((D_MODEL, 2 * D_MODEL)),
                  _resident((1, D_MODEL)),
                  _resident((N_SPATIAL_GROUPS, CHUNK, CHUNK)),
                  _resident((N_SPATIAL_GROUPS, CHUNK, CHUNK)),
                  _resident((D_MODEL, D_MODEL))],
        out_specs=pl.BlockSpec((TM_ROW, D_MODEL), lambda i: (i, 0)),
        out_shape=jax.ShapeDtypeStruct((n, D_MODEL), F32),
        scratch_shapes=[pltpu.VMEM((TM_ROW, D_MODEL), BF16)],
        compiler_params=_params(1),
        name="chunk_gmlp",
    )(x2, g, w_in, g_v, w_s, b_s, w_out)


def _ffn_kernel(xm_ref, xp_ref, xn_ref, g_ref, wup_ref, cw_ref, cb_ref, wdn_ref, out_ref,
                acc_ref, h_ref, a_ref, c0_ref, c1_ref):
    i = pl.program_id(1)
    tm = xm_ref.shape[1]
    seg = tm // SUBLANES
    n_slabs = D_MODEL // LANES
    g = g_ref[...]
    c_bufs = (c0_ref, c1_ref)
    sublane = lax.broadcasted_iota(jnp.int32, (SUBLANES, 2 * FF_CHUNK), 0)

    def load_tile(u):
        for s in range(SUBLANES):
            for slab in range(n_slabs):
                acc_ref[u, slab, pl.ds(s, seg, stride=SUBLANES), :] = (
                    xm_ref[u, s * seg:(s + 1) * seg, slab * LANES:(slab + 1) * LANES])
        xm = jnp.concatenate([acc_ref[u, slab] for slab in range(n_slabs)], axis=1)
        xp = jnp.where(i > 0, xp_ref[u], 0.0)
        xn = jnp.where(i < pl.num_programs(1) - 1, xn_ref[u], 0.0)
        h_ref[u] = jnp.concatenate([_rms(xp, g), _rms(xm, g), _rms(xn, g)], axis=0).astype(BF16)

    def store_tile(u):
        for s in range(SUBLANES):
            for slab in range(n_slabs):
                out_ref[u, s * seg:(s + 1) * seg, slab * LANES:(slab + 1) * LANES] = (
                    acc_ref[u, slab, pl.ds(s, seg, stride=SUBLANES), :])

    def up(u, c, parity):
        cw = cw_ref[c]
        gate_col = pl.multiple_of(c * FF_CHUNK, FF_CHUNK)
        val_col = pl.multiple_of(D_FF + c * FF_CHUNK, LANES)
        a = a_ref.at[u]
        a[:, :FF_CHUNK] = jnp.dot(h_ref[u], wup_ref[:, pl.ds(gate_col, FF_CHUNK)], preferred_element_type=F32)
        a[:, FF_CHUNK:] = jnp.dot(h_ref[u], wup_ref[:, pl.ds(val_col, FF_CHUNK)], preferred_element_type=F32)
        a[0:HALO] = pltpu.roll(
            jnp.where(sublane == SUBLANES - 1, a[0:HALO], a[tm:tm + HALO]), 1, 0)
        a[tm + HALO:tm + 2 * HALO] = pltpu.roll(
            jnp.where(sublane == 0, a[tm + HALO:tm + 2 * HALO], a[HALO:2 * HALO]), SUBLANES - 1, 0)
        c_bufs[parity][u] = (cw[0:1] * a[0:tm, :] + cw[1:2] * a[HALO:HALO + tm, :]
                             + cw[2:3] * a[2 * HALO:2 * HALO + tm, :] + cb_ref[c])

    def down(u, c, parity):
        gate = c_bufs[parity][u, :, :FF_CHUNK]
        val = c_bufs[parity][u, :, FF_CHUNK:]
        act = gate * pl.reciprocal(1.0 + jnp.exp(-gate), approx=True) * val
        y = jnp.dot(act.astype(BF16), wdn_ref[pl.ds(pl.multiple_of(c * FF_CHUNK, FF_CHUNK), FF_CHUNK), :],
                    preferred_element_type=F32)
        for slab in range(n_slabs):
            acc_ref[u, slab] += y[:, slab * LANES:(slab + 1) * LANES]

    streams = range(FFN_STREAMS)
    for u in streams:
        load_tile(u)
    for u in streams:
        up(u, 0, 0)

    def two_chunks(k, carry):
        c = 2 * k
        for u in streams:
            up(u, c + 1, 1)
        for u in streams:
            down(u, c, 0)
        for u in streams:
            up(u, c + 2, 0)
        for u in streams:
            down(u, c + 1, 1)
        return carry

    assert N_FF_CHUNKS % 2 == 1
    lax.fori_loop(0, N_FF_CHUNKS // 2, two_chunks, 0)
    for u in streams:
        down(u, N_FF_CHUNKS - 1, 0)
    for u in streams:
        store_tile(u)


def _ffn(x3, g, w_up, conv_w, conv_b, w_down):
    batch = x3.shape[0]
    tiles = SEQ // TM
    halo_blocks = TM // HALO
    last_halo = SEQ // HALO - 1
    return pl.pallas_call(
        _ffn_kernel,
        grid=(batch // FFN_STREAMS, tiles),
        in_specs=[pl.BlockSpec((FFN_STREAMS, TM, D_MODEL), lambda b, i: (b, i, 0)),
                  pl.BlockSpec((FFN_STREAMS, HALO, D_MODEL),
                               lambda b, i: (b, jnp.maximum(i * halo_blocks - 1, 0), 0)),
                  pl.BlockSpec((FFN_STREAMS, HALO, D_MODEL),
                               lambda b, i: (b, jnp.minimum((i + 1) * halo_blocks, last_halo), 0)),
                  _resident((1, D_MODEL)),
                  _resident((D_MODEL, 2 * D_FF)),
                  _resident((N_FF_CHUNKS, 3, 2 * FF_CHUNK)),
                  _resident((N_FF_CHUNKS, 1, 2 * FF_CHUNK)),
                  _resident((D_FF, D_MODEL))],
        out_specs=pl.BlockSpec((FFN_STREAMS, TM, D_MODEL), lambda b, i: (b, i, 0)),
        out_shape=jax.ShapeDtypeStruct(x3.shape, F32),
        scratch_shapes=[pltpu.VMEM((FFN_STREAMS, D_MODEL // LANES, TM, LANES), F32),
                        pltpu.VMEM((FFN_STREAMS, TM + 2 * HALO, D_MODEL), BF16),
                        pltpu.VMEM((FFN_STREAMS, TM + 2 * HALO, 2 * FF_CHUNK), F32),
                        pltpu.VMEM((FFN_STREAMS, TM, 2 * FF_CHUNK), F32),
                        pltpu.VMEM((FFN_STREAMS, TM, 2 * FF_CHUNK), F32)],
        compiler_params=_params(2),
        name="conv_gated_ffn",
    )(x3, x3, x3, g, w_up, conv_w, conv_b, w_down)


def _qkv_column_order():
    cols = []
    for group in QKV_COLUMN_GROUPS:
        for pair in range(2):
            for part in range(3):
                for h2 in range(2):
                    head = group * HEADS_PER_GROUP + 2 * pair + h2
                    start = part * ATTN_WIDTH + head * HEAD_DIM
                    cols.extend(range(start, start + HEAD_DIM))
    return np.asarray(cols, dtype=np.int32)


def _chunk_gate_val(t):
    lead = t.shape[:-1]
    gv = t.reshape(lead + (2, N_FF_CHUNKS, FF_CHUNK))
    gv = jnp.moveaxis(gv, -2, 0)
    return gv.reshape((N_FF_CHUNKS,) + lead + (2 * FF_CHUNK,))


def _prep_ffn(layer, ffn_norm_g, ffn_w_up, ffn_conv_w, ffn_conv_b, ffn_w_down):
    return (ffn_norm_g[layer][None, :],
            ffn_w_up[layer].astype(BF16),
            _chunk_gate_val(ffn_conv_w[layer]),
            _chunk_gate_val(ffn_conv_b[layer][None, :]),
            ffn_w_down[layer].astype(BF16))


def _trunk(x, attn_w, gmlp_w, ffn_w0, ffn_w1):
    batch = x.shape[0]
    n = batch * SEQ

    norm_g, w_qkv, block_avg, qk_gain, w_o = attn_w
    qkvs = _qkv_proj(x, norm_g, w_qkv, block_avg, qk_gain)
    os = [_attention_group(qkv_g, group) for group, qkv_g in enumerate(qkvs)]
    x = _attn_out(x, os, w_o)
    x = _ffn(x, *ffn_w0)

    x2 = _gmlp(x.reshape(n, D_MODEL), *gmlp_w)
    return _ffn(x2.reshape(batch, SEQ, D_MODEL), *ffn_w1)


def kernel(x_prompt, x_sample, attn_norm_g, attn_w_qkv, attn_q_norm_g, attn_k_norm_g, attn_w_o, gmlp_norm_g, gmlp_w_in, gmlp_v_norm_g, gmlp_w_spatial, gmlp_b_spatial, gmlp_w_out, ffn_norm_g, ffn_w_up, ffn_conv_w, ffn_conv_b, ffn_w_down):
    head_of = np.arange(2 * LANES) // HEAD_DIM
    block_avg = jnp.asarray((head_of[:, None] == head_of[None, :]) / HEAD_DIM, dtype=BF16)
    qk_gain = jnp.concatenate([jnp.tile(attn_q_norm_g[0] * HEAD_DIM ** -0.5, 2),
                               jnp.tile(attn_k_norm_g[0], 2)])[None, :]
    attn_w = (attn_norm_g[0][None, :],
              attn_w_qkv[0][:, _qkv_column_order()].astype(BF16),
              block_avg, qk_gain,
              attn_w_o[0].astype(BF16))
    gmlp_w = (gmlp_norm_g[0][None, :],
              gmlp_w_in[0].astype(BF16),
              gmlp_v_norm_g[0][None, :],
              gmlp_w_spatial[0].astype(BF16),
              jnp.broadcast_to(gmlp_b_spatial[0][:, :, None], (N_SPATIAL_GROUPS, CHUNK, CHUNK)),
              gmlp_w_out[0].astype(BF16))
    ffn_w0 = _prep_ffn(0, ffn_norm_g, ffn_w_up, ffn_conv_w, ffn_conv_b, ffn_w_down)
    ffn_w1 = _prep_ffn(1, ffn_norm_g, ffn_w_up, ffn_conv_w, ffn_conv_b, ffn_w_down)
    return (_trunk(x_prompt, attn_w, gmlp_w, ffn_w0, ffn_w1),
            _trunk(x_sample, attn_w, gmlp_w, ffn_w0, ffn_w1))
```

```python
import functools

import numpy as np
import jax
import jax.numpy as jnp
from jax import lax
from jax.experimental import pallas as pl
from jax.experimental.pallas import tpu as pltpu

F32 = jnp.float32
BF16 = jnp.bfloat16

D_MODEL = 1024
SEQ = 4096
HEAD_DIM = 64
DILATIONS = (1, 4, 16)
N_GROUPS = len(DILATIONS)
HALF_WINDOW = 64
HEADS_PER_GROUP = 4
N_HEADS = N_GROUPS * HEADS_PER_GROUP
ATTN_WIDTH = N_HEADS * HEAD_DIM
GROUP_WIDTH = HEADS_PER_GROUP * HEAD_DIM
LANES = 128
PAIR_WIDTH = 2 * HEAD_DIM
assert PAIR_WIDTH == LANES
QKV_GROUP_WIDTH = 3 * GROUP_WIDTH
QKV_COLUMN_GROUPS = (2, 1, 0)
O_GROUP_WIDTH = GROUP_WIDTH + LANES
LSE_LANES = 2
CHUNK = 128
N_SPATIAL_GROUPS = 8
D_FF = 2816
EPS = 1e-6
NEG_INF = -1e30

Q_BLOCK = 128
K_BLOCK = Q_BLOCK + 2 * HALF_WINDOW
ATTN_UNROLL = 16
FF_CHUNK = 256
N_FF_CHUNKS = D_FF // FF_CHUNK
SUBLANES = 8
HALO = SUBLANES
TM = 512
TM_ROW = 1024
FFN_STREAMS = 2
VMEM_LIMIT = 56 * 1024 * 1024


def _rms(x, g):
    ms = jnp.mean(x * x, axis=-1, keepdims=True)
    return x * lax.rsqrt(ms + EPS) * g


def _params(n_axes):
    return pltpu.CompilerParams(dimension_semantics=("arbitrary",) * n_axes,
                                vmem_limit_bytes=VMEM_LIMIT)


def _resident(shape):
    nd = len(shape)
    return pl.BlockSpec(shape, lambda *_: (0,) * nd, pipeline_mode=pl.Buffered(1))


def _residue_major_spec(dilation, rows, width):
    return pl.BlockSpec((None, dilation, rows // dilation, width), lambda b, i: (b, 0, i, 0))


def _qkv_kernel(x_ref, g_ref, w_ref, bd_ref, gv_ref, o0_ref, o1_ref, o2_ref, y_ref):
    tm = x_ref.shape[0]
    h = _rms(x_ref[...], g_ref[...]).astype(BF16)
    out_refs = (o0_ref, o1_ref, o2_ref)
    y_all = jnp.dot(h, w_ref[...], preferred_element_type=F32)
    for position, group in enumerate(QKV_COLUMN_GROUPS):
        dilation = DILATIONS[group]
        for pair in range(2):
            col0 = position * QKV_GROUP_WIDTH + pair * 3 * LANES
            y = y_all[:, col0:col0 + 3 * LANES]
            qk = y[:, :2 * LANES]
            ms = jnp.dot((qk * qk).astype(BF16), bd_ref[...], preferred_element_type=F32)
            qkn = qk * lax.rsqrt(ms + EPS) * gv_ref[...]
            slabs = (qkn[:, :LANES], qkn[:, LANES:], y[:, 2 * LANES:])
            out_ref = out_refs[group]
            for s, slab in enumerate(slabs):
                cols = slice((pair * 3 + s) * LANES, (pair * 3 + s + 1) * LANES)
                if dilation == 1:
                    out_ref[0, :, cols] = slab.astype(BF16)
                else:
                    slot = ((group - 1) * 2 + pair) * 3 + s
                    y_ref[slot] = slab
                    for rho in range(dilation):
                        rows = y_ref[slot, pl.ds(rho, tm // dilation, stride=dilation), :]
                        out_ref[rho, :, cols] = rows.astype(BF16)


def _qkv_proj(x3, g, w, bd, gv):
    batch = x3.shape[0]
    n_strided_slabs = (N_GROUPS - 1) * 2 * 3
    return pl.pallas_call(
        _qkv_kernel,
        grid=(batch, SEQ // TM_ROW),
        in_specs=[pl.BlockSpec((None, TM_ROW, D_MODEL), lambda b, i: (b, i, 0)),
                  _resident((1, D_MODEL)),
                  _resident((D_MODEL, 3 * ATTN_WIDTH)),
                  _resident((2 * LANES, 2 * LANES)),
                  _resident((1, 2 * LANES))],
        out_specs=[_residue_major_spec(r, TM_ROW, QKV_GROUP_WIDTH) for r in DILATIONS],
        out_shape=[jax.ShapeDtypeStruct((batch, r, SEQ // r, QKV_GROUP_WIDTH), BF16) for r in DILATIONS],
        scratch_shapes=[pltpu.VMEM((n_strided_slabs, TM_ROW, LANES), F32)],
        compiler_params=_params(2),
        name="qkv_proj",
    )(x3, g, w, bd, gv)


LOG2_E = float(np.log2(np.e))


def _alibi_slope(head):
    return float(np.exp2(-8.0 * (head + 1) / N_HEADS)) * LOG2_E


def _attn_kernel(qkv_ref, o_ref, bias_ref, *, sub_len, dilation, group):
    n_blocks = sub_len // Q_BLOCK
    assert n_blocks & (n_blocks - 1) == 0
    offsets = (0, HALF_WINDOW, 2 * HALF_WINDOW)

    @pl.when(pl.program_id(0) == 0)
    def _build_bias():
        row = lax.broadcasted_iota(jnp.int32, (2 * Q_BLOCK, K_BLOCK), 0)
        key = lax.broadcasted_iota(jnp.int32, (2 * Q_BLOCK, K_BLOCK), 1)
        query = row & (Q_BLOCK - 1)
        for variant, off in enumerate(offsets):
            rel = jnp.abs(key - off - query)
            dist = (dilation * rel).astype(F32)
            for pair in range(2):
                head = group * HEADS_PER_GROUP + 2 * pair
                slope = jnp.where(row < Q_BLOCK, _alibi_slope(head), _alibi_slope(head + 1))
                bias_ref[variant, pair] = jnp.where(rel <= HALF_WINDOW, -(slope * dist), NEG_INF)

    lane = lax.broadcasted_iota(jnp.int32, (Q_BLOCK, LANES), 1)
    first_head = lane < HEAD_DIM
    lane_head = lane >> (LSE_LANES.bit_length() - 1)
    ones = jnp.ones((K_BLOCK, LANES), BF16)

    def block(unit, carry):
        rho = lax.shift_right_logical(unit, n_blocks.bit_length() - 1)
        n = unit & (n_blocks - 1)
        q_start = pl.multiple_of(n * Q_BLOCK, Q_BLOCK)
        k_start = pl.multiple_of(jnp.clip(n * Q_BLOCK - HALF_WINDOW, 0, sub_len - K_BLOCK), HALF_WINDOW)
        variant = jnp.where(n == 0, 0, jnp.where(n == n_blocks - 1, 2, 1))
        lse_slab = jnp.zeros((Q_BLOCK, LANES), F32)
        for pair in range(2):
            base = pair * 3 * LANES
            q = qkv_ref[rho, pl.ds(q_start, Q_BLOCK), base:base + LANES]
            k = qkv_ref[rho, pl.ds(k_start, K_BLOCK), base + LANES:base + 2 * LANES]
            v = qkv_ref[rho, pl.ds(k_start, K_BLOCK), base + 2 * LANES:base + 3 * LANES]
            zero = jnp.zeros_like(q)
            qq = jnp.concatenate([jnp.where(first_head, q, zero), jnp.where(first_head, zero, q)], axis=0)
            s = lax.dot_general(qq, k, (((1,), (1,)), ((), ())), preferred_element_type=F32)
            s = s + bias_ref[variant, pair]
            m = jnp.max(s, axis=1, keepdims=True)
            e = jnp.exp2(s - m).astype(BF16)
            pv = jnp.dot(e, jnp.concatenate([v, ones], axis=1), preferred_element_type=F32)
            l = pv[:, LANES:]
            o2 = pv[:, :LANES] * pl.reciprocal(l, approx=True)
            o_pair = jnp.where(first_head, o2[:Q_BLOCK], o2[Q_BLOCK:])
            o_ref[rho, pl.ds(q_start, Q_BLOCK), pair * LANES:(pair + 1) * LANES] = o_pair.astype(BF16)
            lse = m + jnp.log2(l)
            for h2 in range(2):
                lse_slab = jnp.where(lane_head == 2 * pair + h2, lse[h2 * Q_BLOCK:(h2 + 1) * Q_BLOCK], lse_slab)
        lse_slab = lse_slab * (1.0 / LOG2_E)
        head_part = lse_slab.astype(BF16).astype(F32)
        lse_slab = jnp.where((lane & 1) == 1, lse_slab - head_part, lse_slab)
        o_ref[rho, pl.ds(q_start, Q_BLOCK), GROUP_WIDTH:] = lse_slab.astype(BF16)
        return carry

    lax.fori_loop(0, dilation * n_blocks, block, 0, unroll=ATTN_UNROLL)


def _attention_group(qkv_g, group):
    batch = qkv_g.shape[0]
    dilation = DILATIONS[group]
    sub_len = SEQ // dilation
    kern = functools.partial(_attn_kernel, sub_len=sub_len, dilation=dilation, group=group)
    return pl.pallas_call(
        kern,
        grid=(batch,),
        in_specs=[pl.BlockSpec((None, dilation, sub_len, QKV_GROUP_WIDTH), lambda b: (b, 0, 0, 0))],
        out_specs=pl.BlockSpec((None, dilation, sub_len, O_GROUP_WIDTH), lambda b: (b, 0, 0, 0)),
        out_shape=jax.ShapeDtypeStruct((batch, dilation, sub_len, O_GROUP_WIDTH), BF16),
        scratch_shapes=[pltpu.VMEM((3, 2, 2 * Q_BLOCK, K_BLOCK), F32)],
        compiler_params=_params(1),
        name=f"dilated_attn_g{group}",
    )(qkv_g)


def _attn_out_kernel(x_ref, o0_ref, o1_ref, o2_ref, w_ref, out_ref, tok_ref):
    tm = x_ref.shape[0]
    n_slabs = O_GROUP_WIDTH // LANES
    slabs = []
    for group, (o_ref, dilation) in enumerate(zip((o0_ref, o1_ref, o2_ref), DILATIONS)):
        for s in range(n_slabs):
            cols = slice(s * LANES, (s + 1) * LANES)
            if dilation == 1:
                slabs.append(o_ref[0, :, cols].astype(F32))
            else:
                slot = (group - 1) * n_slabs + s
                for rho in range(dilation):
                    tok_ref[slot, pl.ds(rho, tm // dilation, stride=dilation), :] = o_ref[rho, :, cols].astype(F32)
                slabs.append(tok_ref[slot])
    lses = [slabs[g * n_slabs + n_slabs - 1] for g in range(N_GROUPS)]
    lses = [t + pltpu.roll(t, LANES - 1, 1) for t in lses]
    top = jnp.maximum(jnp.maximum(lses[0], lses[1]), lses[2])
    es = [jnp.exp(t - top) for t in lses]
    inv = 1.0 / (es[0] + es[1] + es[2])
    lane = lax.broadcasted_iota(jnp.int32, (tm, GROUP_WIDTH), 1)
    acc = x_ref[...]
    for g in range(N_GROUPS):
        alpha = es[g] * inv
        col = [alpha[:, LSE_LANES * hg:LSE_LANES * hg + 1] for hg in range(HEADS_PER_GROUP)]
        wide = jnp.where(lane < HEAD_DIM, col[0],
                         jnp.where(lane < 2 * HEAD_DIM, col[1], jnp.where(lane < 3 * HEAD_DIM, col[2], col[3])))
        o = jnp.concatenate(slabs[g * n_slabs:g * n_slabs + n_slabs - 1], axis=1)
        acc = acc + jnp.dot((o * wide).astype(BF16), w_ref[g * GROUP_WIDTH:(g + 1) * GROUP_WIDTH, :],
                            preferred_element_type=F32)
    out_ref[...] = acc


def _attn_out(x3, os, w_o):
    batch = x3.shape[0]
    n_strided_slabs = (N_GROUPS - 1) * (O_GROUP_WIDTH // LANES)
    return pl.pallas_call(
        _attn_out_kernel,
        grid=(batch, SEQ // TM_ROW),
        in_specs=[pl.BlockSpec((None, TM_ROW, D_MODEL), lambda b, i: (b, i, 0))]
                 + [_residue_major_spec(r, TM_ROW, O_GROUP_WIDTH) for r in DILATIONS]
                 + [_resident((ATTN_WIDTH, D_MODEL))],
        out_specs=pl.BlockSpec((None, TM_ROW, D_MODEL), lambda b, i: (b, i, 0)),
        out_shape=jax.ShapeDtypeStruct(x3.shape, F32),
        scratch_shapes=[pltpu.VMEM((n_strided_slabs, TM_ROW, LANES), F32)],
        compiler_params=_params(2),
        name="attn_out_proj",
    )(x3, *os, w_o)


def _gmlp_kernel(x_ref, g_ref, win_ref, gv_ref, ws_ref, bs_ref, wout_ref, out_ref, gated_ref):
    x = x_ref[...]
    h = _rms(x, g_ref[...]).astype(BF16)
    z = jax.nn.gelu(jnp.dot(h, win_ref[...], preferred_element_type=F32))
    u = z[:, :D_MODEL]
    v = _rms(z[:, D_MODEL:], gv_ref[...]).astype(BF16)
    for n in range(x.shape[0] // CHUNK):
        rows = slice(n * CHUNK, (n + 1) * CHUNK)
        for sg in range(N_SPATIAL_GROUPS):
            cols = slice(sg * CHUNK, (sg + 1) * CHUNK)
            s = jnp.dot(ws_ref[sg], v[rows, cols], preferred_element_type=F32) + bs_ref[sg]
            gated_ref[rows, cols] = (u[rows, cols] * s).astype(BF16)
    out_ref[...] = x + jnp.dot(gated_ref[...], wout_ref[...], preferred_element_type=F32)


def _gmlp(x2, g, w_in, g_v, w_s, b_s, w_out):
    n = x2.shape[0]
    return pl.pallas_call(
        _gmlp_kernel,
        grid=(n // TM_ROW,),
        in_specs=[pl.BlockSpec((TM_ROW, D_MODEL), lambda i: (i, 0)),
                  _resident((1, D_MODEL)),
                  _resident((D_MODEL, 2 * D_MODEL)),
                  _resident((1, D_MODEL)),
                  _resident((N_SPATIAL_GROUPS, CHUNK, CHUNK)),
                  _resident((N_SPATIAL_GROUPS, CHUNK, CHUNK)),
                  _resident((D_MODEL, D_MODEL))],
        out_specs=pl.BlockSpec((TM_ROW, D_MODEL), lambda i: (i, 0)),
        out_shape=jax.ShapeDtypeStruct((n, D_MODEL), F32),
        scratch_shapes=[pltpu.VMEM((TM_ROW, D_MODEL), BF16)],
        compiler_params=_params(1),
        name="chunk_gmlp",
    )(x2, g, w_in, g_v, w_s, b_s, w_out)


def _ffn_kernel(xm_ref, xp_ref, xn_ref, g_ref, wup_ref, cw_ref, cb_ref, wdn_ref, out_ref,
                acc_ref, h_ref, a_ref, c0_ref, c1_ref):
    i = pl.program_id(1)
    tm = xm_ref.shape[1]
    seg = tm // SUBLANES
    n_slabs = D_MODEL // LANES
    g = g_ref[...]
    c_bufs = (c0_ref, c1_ref)
    sublane = lax.broadcasted_iota(jnp.int32, (SUBLANES, 2 * FF_CHUNK), 0)

    def load_tile(u):
        for s in range(SUBLANES):
            for slab in range(n_slabs):
                acc_ref[u, slab, pl.ds(s, seg, stride=SUBLANES), :] = (
                    xm_ref[u, s * seg:(s + 1) * seg, slab * LANES:(slab + 1) * LANES])
        xm = jnp.concatenate([acc_ref[u, slab] for slab in range(n_slabs)], axis=1)
        xp = jnp.where(i > 0, xp_ref[u], 0.0)
        xn = jnp.where(i < pl.num_programs(1) - 1, xn_ref[u], 0.0)
        h_ref[u] = jnp.concatenate([_rms(xp, g), _rms(xm, g), _rms(xn, g)], axis=0).astype(BF16)

    def store_tile(u):
        for s in range(SUBLANES):
            for slab in range(n_slabs):
                out_ref[u, s * seg:(s + 1) * seg, slab * LANES:(slab + 1) * LANES] = (
                    acc_ref[u, slab, pl.ds(s, seg, stride=SUBLANES), :])

    def up(u, c, parity):
        cw = cw_ref[c]
        gate_col = pl.multiple_of(c * FF_CHUNK, FF_CHUNK)
        val_col = pl.multiple_of(D_FF + c * FF_CHUNK, LANES)
        a = a_ref.at[u]
        a[:, :FF_CHUNK] = jnp.dot(h_ref[u], wup_ref[:, pl.ds(gate_col, FF_CHUNK)], preferred_element_type=F32)
        a[:, FF_CHUNK:] = jnp.dot(h_ref[u], wup_ref[:, pl.ds(val_col, FF_CHUNK)], preferred_element_type=F32)
        a[0:HALO] = pltpu.roll(
            jnp.where(sublane == SUBLANES - 1, a[0:HALO], a[tm:tm + HALO]), 1, 0)
        a[tm + HALO:tm + 2 * HALO] = pltpu.roll(
            jnp.where(sublane == 0, a[tm + HALO:tm + 2 * HALO], a[HALO:2 * HALO]), SUBLANES - 1, 0)
        c_bufs[parity][u] = (cw[0:1] * a[0:tm, :] + cw[1:2] * a[HALO:HALO + tm, :]
                             + cw[2:3] * a[2 * HALO:2 * HALO + tm, :] + cb_ref[c])

    def down(u, c, parity):
        gate = c_bufs[parity][u, :, :FF_CHUNK]
        val = c_bufs[parity][u, :, FF_CHUNK:]
        act = gate * pl.reciprocal(1.0 + jnp.exp(-gate), approx=True) * val
        y = jnp.dot(act.astype(BF16), wdn_ref[pl.ds(pl.multiple_of(c * FF_CHUNK, FF_CHUNK), FF_CHUNK), :],
                    preferred_element_type=F32)
        for slab in range(n_slabs):
            acc_ref[u, slab] += y[:, slab * LANES:(slab + 1) * LANES]

    streams = range(FFN_STREAMS)
    for u in streams:
        load_tile(u)
    for u in streams:
        up(u, 0, 0)

    def two_chunks(k, carry):
        c = 2 * k
        for u in streams:
            up(u, c + 1, 1)
        for u in streams:
            down(u, c, 0)
        for u in streams:
            up(u, c + 2, 0)
        for u in streams:
            down(u, c + 1, 1)
        return carry

    assert N_FF_CHUNKS % 2 == 1
    lax.fori_loop(0, N_FF_CHUNKS // 2, two_chunks, 0)
    for u in streams:
        down(u, N_FF_CHUNKS - 1, 0)
    for u in streams:
        store_tile(u)


def _ffn(x3, g, w_up, conv_w, conv_b, w_down):
    batch = x3.shape[0]
    tiles = SEQ // TM
    halo_blocks = TM // HALO
    last_halo = SEQ // HALO - 1
    return pl.pallas_call(
        _ffn_kernel,
        grid=(batch // FFN_STREAMS, tiles),
        in_specs=[pl.BlockSpec((FFN_STREAMS, TM, D_MODEL), lambda b, i: (b, i, 0)),
                  pl.BlockSpec((FFN_STREAMS, HALO, D_MODEL),
                               lambda b, i: (b, jnp.maximum(i * halo_blocks - 1, 0), 0)),
                  pl.BlockSpec((FFN_STREAMS, HALO, D_MODEL),
                               lambda b, i: (b, jnp.minimum((i + 1) * halo_blocks, last_halo), 0)),
                  _resident((1, D_MODEL)),
                  _resident((D_MODEL, 2 * D_FF)),
                  _resident((N_FF_CHUNKS, 3, 2 * FF_CHUNK)),
                  _resident((N_FF_CHUNKS, 1, 2 * FF_CHUNK)),
                  _resident((D_FF, D_MODEL))],
        out_specs=pl.BlockSpec((FFN_STREAMS, TM, D_MODEL), lambda b, i: (b, i, 0)),
        out_shape=jax.ShapeDtypeStruct(x3.shape, F32),
        scratch_shapes=[pltpu.VMEM((FFN_STREAMS, D_MODEL // LANES, TM, LANES), F32),
                        pltpu.VMEM((FFN_STREAMS, TM + 2 * HALO, D_MODEL), BF16),
                        pltpu.VMEM((FFN_STREAMS, TM + 2 * HALO, 2 * FF_CHUNK), F32),
                        pltpu.VMEM((FFN_STREAMS, TM, 2 * FF_CHUNK), F32),
                        pltpu.VMEM((FFN_STREAMS, TM, 2 * FF_CHUNK), F32)],
        compiler_params=_params(2),
        name="conv_gated_ffn",
    )(x3, x3, x3, g, w_up, conv_w, conv_b, w_down)


def _qkv_column_order():
    cols = []
    for group in QKV_COLUMN_GROUPS:
        for pair in range(2):
            for part in range(3):
                for h2 in range(2):
                    head = group * HEADS_PER_GROUP + 2 * pair + h2
                    start = part * ATTN_WIDTH + head * HEAD_DIM
                    cols.extend(range(start, start + HEAD_DIM))
    return np.asarray(cols, dtype=np.int32)


def _chunk_gate_val(t):
    lead = t.shape[:-1]
    gv = t.reshape(lead + (2, N_FF_CHUNKS, FF_CHUNK))
    gv = jnp.moveaxis(gv, -2, 0)
    return gv.reshape((N_FF_CHUNKS,) + lead + (2 * FF_CHUNK,))


def _prep_ffn(layer, ffn_norm_g, ffn_w_up, ffn_conv_w, ffn_conv_b, ffn_w_down):
    return (ffn_norm_g[layer][None, :],
            ffn_w_up[layer].astype(BF16),
            _chunk_gate_val(ffn_conv_w[layer]),
            _chunk_gate_val(ffn_conv_b[layer][None, :]),
            ffn_w_down[layer].astype(BF16))


def _trunk(x, attn_w, gmlp_w, ffn_w0, ffn_w1):
    batch = x.shape[0]
    n = batch * SEQ

    norm_g, w_qkv, block_avg, qk_gain, w_o = attn_w
    qkvs = _qkv_proj(x, norm_g, w_qkv, block_avg, qk_gain)
    os = [_attention_group(qkv_g, group) for group, qkv_g in enumerate(qkvs)]
    x = _attn_out(x, os, w_o)
    x = _ffn(x, *ffn_w0)

    x2 = _gmlp(x.reshape(n, D_MODEL), *gmlp_w)
    return _ffn(x2.reshape(batch, SEQ, D_MODEL), *ffn_w1)


def kernel(x_prompt, x_sample, attn_norm_g, attn_w_qkv, attn_q_norm_g, attn_k_norm_g, attn_w_o, gmlp_norm_g, gmlp_w_in, gmlp_v_norm_g, gmlp_w_spatial, gmlp_b_spatial, gmlp_w_out, ffn_norm_g, ffn_w_up, ffn_conv_w, ffn_conv_b, ffn_w_down):
    head_of = np.arange(2 * LANES) // HEAD_DIM
    block_avg = jnp.asarray((head_of[:, None] == head_of[None, :]) / HEAD_DIM, dtype=BF16)
    qk_gain = jnp.concatenate([jnp.tile(attn_q_norm_g[0] * (HEAD_DIM ** -0.5 * LOG2_E), 2),
                               jnp.tile(attn_k_norm_g[0], 2)])[None, :]
    attn_w = (attn_norm_g[0][None, :],
              attn_w_qkv[0][:, _qkv_column_order()].astype(BF16),
              block_avg, qk_gain,
              attn_w_o[0].astype(BF16))
    gmlp_w = (gmlp_norm_g[0][None, :],
              gmlp_w_in[0].astype(BF16),
              gmlp_v_norm_g[0][None, :],
              gmlp_w_spatial[0].astype(BF16),
              jnp.broadcast_to(gmlp_b_spatial[0][:, :, None], (N_SPATIAL_GROUPS, CHUNK, CHUNK)),
              gmlp_w_out[0].astype(BF16))
    ffn_w0 = _prep_ffn(0, ffn_norm_g, ffn_w_up, ffn_conv_w, ffn_conv_b, ffn_w_down)
    ffn_w1 = _prep_ffn(1, ffn_norm_g, ffn_w_up, ffn_conv_w, ffn_conv_b, ffn_w_down)
    return (_trunk(x_prompt, attn_w, gmlp_w, ffn_w0, ffn_w1),
            _trunk(x_sample, attn_w, gmlp_w, ffn_w0, ffn_w1))
```

```python
import functools

import numpy as np
import jax
import jax.numpy as jnp
from jax import lax
from jax.experimental import pallas as pl
from jax.experimental.pallas import tpu as pltpu

F32 = jnp.float32
BF16 = jnp.bfloat16

D_MODEL = 1024
SEQ = 4096
HEAD_DIM = 64
DILATIONS = (1, 4, 16)
N_GROUPS = len(DILATIONS)
HALF_WINDOW = 64
HEADS_PER_GROUP = 4
N_HEADS = N_GROUPS * HEADS_PER_GROUP
ATTN_WIDTH = N_HEADS * HEAD_DIM
GROUP_WIDTH = HEADS_PER_GROUP * HEAD_DIM
LANES = 128
PAIR_WIDTH = 2 * HEAD_DIM
assert PAIR_WIDTH == LANES
QKV_GROUP_WIDTH = 3 * GROUP_WIDTH
QKV_COLUMN_GROUPS = (2, 1, 0)
O_GROUP_WIDTH = GROUP_WIDTH + LANES
LSE_LANES = 2
CHUNK = 128
N_SPATIAL_GROUPS = 8
D_FF = 2816
EPS = 1e-6
NEG_INF = -1e30

Q_BLOCK = 128
K_BLOCK = Q_BLOCK + 2 * HALF_WINDOW
ATTN_UNROLL = 16
FF_CHUNK = 256
N_FF_CHUNKS = D_FF // FF_CHUNK
SUBLANES = 8
HALO = SUBLANES
TM = 512
TM_ROW = 1024
FFN_STREAMS = 2
VMEM_LIMIT = 56 * 1024 * 1024


def _rms(x, g):
    ms = jnp.mean(x * x, axis=-1, keepdims=True)
    return x * lax.rsqrt(ms + EPS) * g


def _params(n_axes):
    return pltpu.CompilerParams(dimension_semantics=("arbitrary",) * n_axes,
                                vmem_limit_bytes=VMEM_LIMIT)


def _resident(shape):
    nd = len(shape)
    return pl.BlockSpec(shape, lambda *_: (0,) * nd, pipeline_mode=pl.Buffered(1))


def _residue_major_spec(dilation, rows, width):
    return pl.BlockSpec((None, dilation, rows // dilation, width), lambda b, i: (b, 0, i, 0))


def _qkv_kernel(x_ref, g_ref, w_ref, bd_ref, gv_ref, o0_ref, o1_ref, o2_ref, y_ref):
    tm = x_ref.shape[0]
    h = _rms(x_ref[...], g_ref[...]).astype(BF16)
    out_refs = (o0_ref, o1_ref, o2_ref)
    y_all = jnp.dot(h, w_ref[...], preferred_element_type=F32)
    for position, group in enumerate(QKV_COLUMN_GROUPS):
        dilation = DILATIONS[group]
        for pair in range(2):
            col0 = position * QKV_GROUP_WIDTH + pair * 3 * LANES
            y = y_all[:, col0:col0 + 3 * LANES]
            qk = y[:, :2 * LANES]
            ms = jnp.dot((qk * qk).astype(BF16), bd_ref[...], preferred_element_type=F32)
            qkn = qk * lax.rsqrt(ms + EPS) * gv_ref[...]
            slabs = (qkn[:, :LANES], qkn[:, LANES:], y[:, 2 * LANES:])
            out_ref = out_refs[group]
            for s, slab in enumerate(slabs):
                cols = slice((pair * 3 + s) * LANES, (pair * 3 + s + 1) * LANES)
                if dilation == 1:
                    out_ref[0, :, cols] = slab.astype(BF16)
                else:
                    slot = ((group - 1) * 2 + pair) * 3 + s
                    y_ref[slot] = slab
                    for rho in range(dilation):
                        rows = y_ref[slot, pl.ds(rho, tm // dilation, stride=dilation), :]
                        out_ref[rho, :, cols] = rows.astype(BF16)


def _qkv_proj(x3, g, w, bd, gv):
    batch = x3.shape[0]
    n_strided_slabs = (N_GROUPS - 1) * 2 * 3
    return pl.pallas_call(
        _qkv_kernel,
        grid=(batch, SEQ // TM_ROW),
        in_specs=[pl.BlockSpec((None, TM_ROW, D_MODEL), lambda b, i: (b, i, 0)),
                  _resident((1, D_MODEL)),
                  _resident((D_MODEL, 3 * ATTN_WIDTH)),
                  _resident((2 * LANES, 2 * LANES)),
                  _resident((1, 2 * LANES))],
        out_specs=[_residue_major_spec(r, TM_ROW, QKV_GROUP_WIDTH) for r in DILATIONS],
        out_shape=[jax.ShapeDtypeStruct((batch, r, SEQ // r, QKV_GROUP_WIDTH), BF16) for r in DILATIONS],
        scratch_shapes=[pltpu.VMEM((n_strided_slabs, TM_ROW, LANES), F32)],
        compiler_params=_params(2),
        name="qkv_proj",
    )(x3, g, w, bd, gv)


LOG2_E = float(np.log2(np.e))


def _alibi_slope(head):
    return float(np.exp2(-8.0 * (head + 1) / N_HEADS)) * LOG2_E


def _attn_kernel(qkv_ref, o_ref, bias_ref, *, sub_len, dilation, group):
    n_blocks = sub_len // Q_BLOCK
    assert n_blocks & (n_blocks - 1) == 0
    offsets = (0, HALF_WINDOW, 2 * HALF_WINDOW)

    @pl.when(pl.program_id(0) == 0)
    def _build_bias():
        row = lax.broadcasted_iota(jnp.int32, (2 * Q_BLOCK, K_BLOCK), 0)
        key = lax.broadcasted_iota(jnp.int32, (2 * Q_BLOCK, K_BLOCK), 1)
        query = row & (Q_BLOCK - 1)
        for variant, off in enumerate(offsets):
            rel = jnp.abs(key - off - query)
            dist = (dilation * rel).astype(F32)
            for pair in range(2):
                head = group * HEADS_PER_GROUP + 2 * pair
                slope = jnp.where(row < Q_BLOCK, _alibi_slope(head), _alibi_slope(head + 1))
                bias_ref[variant, pair] = jnp.where(rel <= HALF_WINDOW, -(slope * dist), NEG_INF)

    lane = lax.broadcasted_iota(jnp.int32, (Q_BLOCK, LANES), 1)
    first_head = lane < HEAD_DIM
    lane_head = lane >> (LSE_LANES.bit_length() - 1)
    ones = jnp.ones((K_BLOCK, LANES), BF16)

    def block(unit, carry):
        rho = lax.shift_right_logical(unit, n_blocks.bit_length() - 1)
        n = unit & (n_blocks - 1)
        q_start = pl.multiple_of(n * Q_BLOCK, Q_BLOCK)
        k_start = pl.multiple_of(jnp.clip(n * Q_BLOCK - HALF_WINDOW, 0, sub_len - K_BLOCK), HALF_WINDOW)
        variant = jnp.where(n == 0, 0, jnp.where(n == n_blocks - 1, 2, 1))
        lse_slab = jnp.zeros((Q_BLOCK, LANES), F32)
        for pair in range(2):
            base = pair * 3 * LANES
            q = qkv_ref[rho, pl.ds(q_start, Q_BLOCK), base:base + LANES]
            k = qkv_ref[rho, pl.ds(k_start, K_BLOCK), base + LANES:base + 2 * LANES]
            v = qkv_ref[rho, pl.ds(k_start, K_BLOCK), base + 2 * LANES:base + 3 * LANES]
            zero = jnp.zeros_like(q)
            qq = jnp.concatenate([jnp.where(first_head, q, zero), jnp.where(first_head, zero, q)], axis=0)
            s = lax.dot_general(qq, k, (((1,), (1,)), ((), ())), preferred_element_type=F32)
            s = s + bias_ref[variant, pair]
            m = jnp.max(s, axis=1, keepdims=True)
            e = jnp.exp2(s - m).astype(BF16)
            pv = jnp.dot(e, jnp.concatenate([v, ones], axis=1), preferred_element_type=F32)
            l = pv[:, LANES:]
            o2 = pv[:, :LANES] * pl.reciprocal(l, approx=True)
            o_pair = jnp.where(first_head, o2[:Q_BLOCK], o2[Q_BLOCK:])
            o_ref[rho, pl.ds(q_start, Q_BLOCK), pair * LANES:(pair + 1) * LANES] = o_pair.astype(BF16)
            lse = m + jnp.log2(l)
            for h2 in range(2):
                lse_slab = jnp.where(lane_head == 2 * pair + h2, lse[h2 * Q_BLOCK:(h2 + 1) * Q_BLOCK], lse_slab)
        lse_slab = lse_slab * (1.0 / LOG2_E)
        head_part = lse_slab.astype(BF16).astype(F32)
        lse_slab = jnp.where((lane & 1) == 1, lse_slab - head_part, lse_slab)
        o_ref[rho, pl.ds(q_start, Q_BLOCK), GROUP_WIDTH:] = lse_slab.astype(BF16)
        return carry

    lax.fori_loop(0, dilation * n_blocks, block, 0, unroll=ATTN_UNROLL)


def _attention_group(qkv_g, group):
    batch = qkv_g.shape[0]
    dilation = DILATIONS[group]
    sub_len = SEQ // dilation
    kern = functools.partial(_attn_kernel, sub_len=sub_len, dilation=dilation, group=group)
    return pl.pallas_call(
        kern,
        grid=(batch,),
        in_specs=[pl.BlockSpec((None, dilation, sub_len, QKV_GROUP_WIDTH), lambda b: (b, 0, 0, 0))],
        out_specs=pl.BlockSpec((None, dilation, sub_len, O_GROUP_WIDTH), lambda b: (b, 0, 0, 0)),
        out_shape=jax.ShapeDtypeStruct((batch, dilation, sub_len, O_GROUP_WIDTH), BF16),
        scratch_shapes=[pltpu.VMEM((3, 2, 2 * Q_BLOCK, K_BLOCK), F32)],
        compiler_params=_params(1),
        name=f"dilated_attn_g{group}",
    )(qkv_g)


def _attn_out_kernel(x_ref, o0_ref, o1_ref, o2_ref, w_ref, spread_ref, out_ref, tok_ref):
    tm = x_ref.shape[0]
    n_slabs = O_GROUP_WIDTH // LANES
    slabs = []
    for group, (o_ref, dilation) in enumerate(zip((o0_ref, o1_ref, o2_ref), DILATIONS)):
        for s in range(n_slabs):
            cols = slice(s * LANES, (s + 1) * LANES)
            if dilation == 1:
                slabs.append(o_ref[0, :, cols].astype(F32))
            else:
                slot = (group - 1) * n_slabs + s
                for rho in range(dilation):
                    tok_ref[slot, pl.ds(rho, tm // dilation, stride=dilation), :] = o_ref[rho, :, cols].astype(F32)
                slabs.append(tok_ref[slot])
    lses = [slabs[g * n_slabs + n_slabs - 1] for g in range(N_GROUPS)]
    lses = [t + pltpu.roll(t, LANES - 1, 1) for t in lses]
    top = jnp.maximum(jnp.maximum(lses[0], lses[1]), lses[2])
    es = [jnp.exp(t - top) for t in lses]
    inv = 1.0 / (es[0] + es[1] + es[2])
    acc = x_ref[...]
    for g in range(N_GROUPS):
        alpha = es[g] * inv
        wide = jnp.dot(alpha.astype(BF16), spread_ref[...], preferred_element_type=F32)
        o = jnp.concatenate(slabs[g * n_slabs:g * n_slabs + n_slabs - 1], axis=1)
        acc = acc + jnp.dot((o * wide).astype(BF16), w_ref[g * GROUP_WIDTH:(g + 1) * GROUP_WIDTH, :],
                            preferred_element_type=F32)
    out_ref[...] = acc


def _attn_out(x3, os, w_o):
    batch = x3.shape[0]
    n_strided_slabs = (N_GROUPS - 1) * (O_GROUP_WIDTH // LANES)
    spread = np.zeros((LANES, GROUP_WIDTH), np.float32)
    for hg in range(HEADS_PER_GROUP):
        spread[LSE_LANES * hg, hg * HEAD_DIM:(hg + 1) * HEAD_DIM] = 1.0
    spread = jnp.asarray(spread, dtype=BF16)
    return pl.pallas_call(
        _attn_out_kernel,
        grid=(batch, SEQ // TM_ROW),
        in_specs=[pl.BlockSpec((None, TM_ROW, D_MODEL), lambda b, i: (b, i, 0))]
                 + [_residue_major_spec(r, TM_ROW, O_GROUP_WIDTH) for r in DILATIONS]
                 + [_resident((ATTN_WIDTH, D_MODEL)), _resident((LANES, GROUP_WIDTH))],
        out_specs=pl.BlockSpec((None, TM_ROW, D_MODEL), lambda b, i: (b, i, 0)),
        out_shape=jax.ShapeDtypeStruct(x3.shape, F32),
        scratch_shapes=[pltpu.VMEM((n_strided_slabs, TM_ROW, LANES), F32)],
        compiler_params=_params(2),
        name="attn_out_proj",
    )(x3, *os, w_o, spread)


def _gmlp_kernel(x_ref, g_ref, win_ref, gv_ref, ws_ref, bs_ref, wout_ref, out_ref, gated_ref):
    x = x_ref[...]
    h = _rms(x, g_ref[...]).astype(BF16)
    z = jax.nn.gelu(jnp.dot(h, win_ref[...], preferred_element_type=F32))
    u = z[:, :D_MODEL]
    v = _rms(z[:, D_MODEL:], gv_ref[...]).astype(BF16)
    for n in range(x.shape[0] // CHUNK):
        rows = slice(n * CHUNK, (n + 1) * CHUNK)
        for sg in range(N_SPATIAL_GROUPS):
            cols = slice(sg * CHUNK, (sg + 1) * CHUNK)
            s = jnp.dot(ws_ref[sg], v[rows, cols], preferred_element_type=F32) + bs_ref[sg]
            gated_ref[rows, cols] = (u[rows, cols] * s).astype(BF16)
    out_ref[...] = x + jnp.dot(gated_ref[...], wout_ref[...], preferred_element_type=F32)


def _gmlp(x2, g, w_in, g_v, w_s, b_s, w_out):
    n = x2.shape[0]
    return pl.pallas_call(
        _gmlp_kernel,
        grid=(n // TM_ROW,),
        in_specs=[pl.BlockSpec((TM_ROW, D_MODEL), lambda i: (i, 0)),
                  _resident((1, D_MODEL)),
                  _resident((D_MODEL, 2 * D_MODEL)),
                  _resident((1, D_MODEL)),
                  _resident((N_SPATIAL_GROUPS, CHUNK, CHUNK)),
                  _resident((N_SPATIAL_GROUPS, CHUNK, CHUNK)),
                  _resident((D_MODEL, D_MODEL))],
        out_specs=pl.BlockSpec((TM_ROW, D_MODEL), lambda i: (i, 0)),
        out_shape=jax.ShapeDtypeStruct((n, D_MODEL), F32),
        scratch_shapes=[pltpu.VMEM((TM_ROW, D_MODEL), BF16)],
        compiler_params=_params(1),
        name="chunk_gmlp",
    )(x2, g, w_in, g_v, w_s, b_s, w_out)


def _ffn_kernel(xm_ref, xp_ref, xn_ref, g_ref, wup_ref, cw_ref, cb_ref, wdn_ref, out_ref,
                acc_ref, h_ref, a_ref, c0_ref, c1_ref):
    i = pl.program_id(1)
    tm = xm_ref.shape[1]
    seg = tm // SUBLANES
    n_slabs = D_MODEL // LANES
    g = g_ref[...]
    c_bufs = (c0_ref, c1_ref)
    sublane = lax.broadcasted_iota(jnp.int32, (SUBLANES, 2 * FF_CHUNK), 0)

    def load_tile(u):
        for s in range(SUBLANES):
            for slab in range(n_slabs):
                acc_ref[u, slab, pl.ds(s, seg, stride=SUBLANES), :] = (
                    xm_ref[u, s * seg:(s + 1) * seg, slab * LANES:(slab + 1) * LANES])
        xm = jnp.concatenate([acc_ref[u, slab] for slab in range(n_slabs)], axis=1)
        xp = jnp.where(i > 0, xp_ref[u], 0.0)
        xn = jnp.where(i < pl.num_programs(1) - 1, xn_ref[u], 0.0)
        h_ref[u] = jnp.concatenate([_rms(xp, g), _rms(xm, g), _rms(xn, g)], axis=0).astype(BF16)

    def store_tile(u):
        for s in range(SUBLANES):
            for slab in range(n_slabs):
                out_ref[u, s * seg:(s + 1) * seg, slab * LANES:(slab + 1) * LANES] = (
                    acc_ref[u, slab, pl.ds(s, seg, stride=SUBLANES), :])

    def up(u, c, parity):
        cw = cw_ref[c]
        gate_col = pl.multiple_of(c * FF_CHUNK, FF_CHUNK)
        val_col = pl.multiple_of(D_FF + c * FF_CHUNK, LANES)
        a = a_ref.at[u]
        a[:, :FF_CHUNK] = jnp.dot(h_ref[u], wup_ref[:, pl.ds(gate_col, FF_CHUNK)], preferred_element_type=F32)
        a[:, FF_CHUNK:] = jnp.dot(h_ref[u], wup_ref[:, pl.ds(val_col, FF_CHUNK)], preferred_element_type=F32)
        a[0:HALO] = pltpu.roll(
            jnp.where(sublane == SUBLANES - 1, a[0:HALO], a[tm:tm + HALO]), 1, 0)
        a[tm + HALO:tm + 2 * HALO] = pltpu.roll(
            jnp.where(sublane == 0, a[tm + HALO:tm + 2 * HALO], a[HALO:2 * HALO]), SUBLANES - 1, 0)
        c_bufs[parity][u] = (cw[0:1] * a[0:tm, :] + cw[1:2] * a[HALO:HALO + tm, :]
                             + cw[2:3] * a[2 * HALO:2 * HALO + tm, :] + cb_ref[c])

    def down(u, c, parity):
        gate = c_bufs[parity][u, :, :FF_CHUNK]
        val = c_bufs[parity][u, :, FF_CHUNK:]
        act = gate * pl.reciprocal(1.0 + jnp.exp(-gate), approx=True) * val
        y = jnp.dot(act.astype(BF16), wdn_ref[pl.ds(pl.multiple_of(c * FF_CHUNK, FF_CHUNK), FF_CHUNK), :],
                    preferred_element_type=F32)
        for slab in range(n_slabs):
            acc_ref[u, slab] += y[:, slab * LANES:(slab + 1) * LANES]

    streams = range(FFN_STREAMS)
    for u in streams:
        load_tile(u)
    for u in streams:
        up(u, 0, 0)

    def two_chunks(k, carry):
        c = 2 * k
        for u in streams:
            up(u, c + 1, 1)
        for u in streams:
            down(u, c, 0)
        for u in streams:
            up(u, c + 2, 0)
        for u in streams:
            down(u, c + 1, 1)
        return carry

    assert N_FF_CHUNKS % 2 == 1
    lax.fori_loop(0, N_FF_CHUNKS // 2, two_chunks, 0)
    for u in streams:
        down(u, N_FF_CHUNKS - 1, 0)
    for u in streams:
        store_tile(u)


def _ffn(x3, g, w_up, conv_w, conv_b, w_down):
    batch = x3.shape[0]
    tiles = SEQ // TM
    halo_blocks = TM // HALO
    last_halo = SEQ // HALO - 1
    return pl.pallas_call(
        _ffn_kernel,
        grid=(batch // FFN_STREAMS, tiles),
        in_specs=[pl.BlockSpec((FFN_STREAMS, TM, D_MODEL), lambda b, i: (b, i, 0)),
                  pl.BlockSpec((FFN_STREAMS, HALO, D_MODEL),
                               lambda b, i: (b, jnp.maximum(i * halo_blocks - 1, 0), 0)),
                  pl.BlockSpec((FFN_STREAMS, HALO, D_MODEL),
                               lambda b, i: (b, jnp.minimum((i + 1) * halo_blocks, last_halo), 0)),
                  _resident((1, D_MODEL)),
                  _resident((D_MODEL, 2 * D_FF)),
                  _resident((N_FF_CHUNKS, 3, 2 * FF_CHUNK)),
                  _resident((N_FF_CHUNKS, 1, 2 * FF_CHUNK)),
                  _resident((D_FF, D_MODEL))],
        out_specs=pl.BlockSpec((FFN_STREAMS, TM, D_MODEL), lambda b, i: (b, i, 0)),
        out_shape=jax.ShapeDtypeStruct(x3.shape, F32),
        scratch_shapes=[pltpu.VMEM((FFN_STREAMS, D_MODEL // LANES, TM, LANES), F32),
                        pltpu.VMEM((FFN_STREAMS, TM + 2 * HALO, D_MODEL), BF16),
                        pltpu.VMEM((FFN_STREAMS, TM + 2 * HALO, 2 * FF_CHUNK), F32),
                        pltpu.VMEM((FFN_STREAMS, TM, 2 * FF_CHUNK), F32),
                        pltpu.VMEM((FFN_STREAMS, TM, 2 * FF_CHUNK), F32)],
        compiler_params=_params(2),
        name="conv_gated_ffn",
    )(x3, x3, x3, g, w_up, conv_w, conv_b, w_down)


def _qkv_column_order():
    cols = []
    for group in QKV_COLUMN_GROUPS:
        for pair in range(2):
            for part in range(3):
                for h2 in range(2):
                    head = group * HEADS_PER_GROUP + 2 * pair + h2
                    start = part * ATTN_WIDTH + head * HEAD_DIM
                    cols.extend(range(start, start + HEAD_DIM))
    return np.asarray(cols, dtype=np.int32)


def _chunk_gate_val(t):
    lead = t.shape[:-1]
    gv = t.reshape(lead + (2, N_FF_CHUNKS, FF_CHUNK))
    gv = jnp.moveaxis(gv, -2, 0)
    return gv.reshape((N_FF_CHUNKS,) + lead + (2 * FF_CHUNK,))


def _prep_ffn(layer, ffn_norm_g, ffn_w_up, ffn_conv_w, ffn_conv_b, ffn_w_down):
    return (ffn_norm_g[layer][None, :],
            ffn_w_up[layer].astype(BF16),
            _chunk_gate_val(ffn_conv_w[layer]),
            _chunk_gate_val(ffn_conv_b[layer][None, :]),
            ffn_w_down[layer].astype(BF16))


def _trunk(x, attn_w, gmlp_w, ffn_w0, ffn_w1):
    batch = x.shape[0]
    n = batch * SEQ

    norm_g, w_qkv, block_avg, qk_gain, w_o = attn_w
    qkvs = _qkv_proj(x, norm_g, w_qkv, block_avg, qk_gain)
    os = [_attention_group(qkv_g, group) for group, qkv_g in enumerate(qkvs)]
    x = _attn_out(x, os, w_o)
    x = _ffn(x, *ffn_w0)

    x2 = _gmlp(x.reshape(n, D_MODEL), *gmlp_w)
    return _ffn(x2.reshape(batch, SEQ, D_MODEL), *ffn_w1)


def kernel(x_prompt, x_sample, attn_norm_g, attn_w_qkv, attn_q_norm_g, attn_k_norm_g, attn_w_o, gmlp_norm_g, gmlp_w_in, gmlp_v_norm_g, gmlp_w_spatial, gmlp_b_spatial, gmlp_w_out, ffn_norm_g, ffn_w_up, ffn_conv_w, ffn_conv_b, ffn_w_down):
    head_of = np.arange(2 * LANES) // HEAD_DIM
    block_avg = jnp.asarray((head_of[:, None] == head_of[None, :]) / HEAD_DIM, dtype=BF16)
    qk_gain = jnp.concatenate([jnp.tile(attn_q_norm_g[0] * (HEAD_DIM ** -0.5 * LOG2_E), 2),
                               jnp.tile(attn_k_norm_g[0], 2)])[None, :]
    attn_w = (attn_norm_g[0][None, :],
              attn_w_qkv[0][:, _qkv_column_order()].astype(BF16),
              block_avg, qk_gain,
              attn_w_o[0].astype(BF16))
    gmlp_w = (gmlp_norm_g[0][None, :],
              gmlp_w_in[0].astype(BF16),
              gmlp_v_norm_g[0][None, :],
              gmlp_w_spatial[0].astype(BF16),
              jnp.broadcast_to(gmlp_b_spatial[0][:, :, None], (N_SPATIAL_GROUPS, CHUNK, CHUNK)),
              gmlp_w_out[0].astype(BF16))
    ffn_w0 = _prep_ffn(0, ffn_norm_g, ffn_w_up, ffn_conv_w, ffn_conv_b, ffn_w_down)
    ffn_w1 = _prep_ffn(1, ffn_norm_g, ffn_w_up, ffn_conv_w, ffn_conv_b, ffn_w_down)
    return (_trunk(x_prompt, attn_w, gmlp_w, ffn_w0, ffn_w1),
            _trunk(x_sample, attn_w, gmlp_w, ffn_w0, ffn_w1))
```
